```python
import math
import jax, jax.numpy as jnp
from jax import lax
import numpy as np

D_MODEL = 4096
BATCH = 4
SEQ = 2048
DEPTH = 2
DEC_BATCH = 16
DEC_SEQ = 16
PAST_LEN = 2048

CHUNK = 64
MIX_W = D_MODEL
LRU_W = MIX_W // 2
N_LRU_BLOCKS = 16
LRU_BLOCK = LRU_W // N_LRU_BLOCKS
CONV_W = 4
LRU_C = 8.0
ATT_W = MIX_W - LRU_W
HEAD_DIM = 128
N_HEADS = ATT_W // HEAD_DIM
N_KV_HEADS = 4
GROUP = N_HEADS // N_KV_HEADS
N_IDX_HEADS = 32
IDX_DIM = 128
TOPK_MAX = 256
PLE_DIM = 256
Q_BLOCK = 128
DN_ALPHA = (2 * DEPTH) ** 0.25
DN_BETA = (8 * DEPTH) ** -0.25
LN_EPS = 1e-5
IN_SPLITS = (LRU_W, LRU_W, N_HEADS * HEAD_DIM, N_KV_HEADS * HEAD_DIM, N_KV_HEADS * HEAD_DIM,
             ATT_W, N_IDX_HEADS * IDX_DIM, IDX_DIM, N_IDX_HEADS)
N_IN = sum(IN_SPLITS)

kernel_name = 'hymba_rglru_dsa_streaming_step'


def _split_cols(z):
    parts = []
    start = 0
    for n in IN_SPLITS:
        parts.append(z[..., start:start + n])
        start += n
    return parts


def _layer_norm(x, g, b):
    xf = x.astype(jnp.float32)
    mu = jnp.mean(xf, axis=-1, keepdims=True)
    var = jnp.mean(jnp.square(xf - mu), axis=-1, keepdims=True)
    return ((xf - mu) * lax.rsqrt(var + LN_EPS) * g + b).astype(x.dtype)


def _causal_conv(u, buf, w, b):
    T = u.shape[1]
    full = jnp.concatenate([buf.astype(u.dtype), u], axis=1)
    y = b + full[:, 0:T] * w[0]
    for j in range(1, CONV_W):
        y = y + full[:, j:j + T] * w[j]
    return y.astype(u.dtype), full[:, -(CONV_W - 1):]


def _rg_lru(u, pos, h0, ga_w, ga_b, gx_w, gx_b, lam):
    B, T, _ = u.shape
    uf = u.astype(jnp.float32)
    ub = uf.reshape(B, T, N_LRU_BLOCKS, LRU_BLOCK)
    r = jax.nn.sigmoid(jnp.einsum('btnd,nde->btne', ub, ga_w).reshape(B, T, LRU_W) + ga_b)
    i = jax.nn.sigmoid(jnp.einsum('btnd,nde->btne', ub, gx_w).reshape(B, T, LRU_W) + gx_b)
    log_a = -LRU_C * r * jax.nn.softplus(-lam.astype(jnp.float32))
    a = jnp.exp(log_a)
    mult = jnp.sqrt(-jnp.expm1(2.0 * log_a))
    mult = jnp.where((pos == 0)[None, :, None], 1.0, mult)
    b = uf * i * mult
    b = b.at[:, 0].add(a[:, 0] * h0.astype(jnp.float32))

    def combine(left, right):
        return (left[0] * right[0], right[0] * left[1] + right[1])

    _, h = lax.associative_scan(combine, (a, b), axis=1)
    return h.astype(u.dtype), h[:, -1].astype(h0.dtype)


def _indexed_sparse_attention(q, qi, wi, q_pos, k, v, ki, k_pos, n_sel):
    B, T = q.shape[:2]
    qblk = Q_BLOCK if T % Q_BLOCK == 0 else T
    nb = T // qblk
    slopes = jnp.exp2(-8.0 * jnp.arange(1, N_HEADS + 1, dtype=jnp.float32) / N_HEADS)
    slopes = slopes.reshape(N_KV_HEADS, GROUP)
    k_chunk = k_pos // CHUNK
    kpos_f = k_pos.astype(jnp.float32)
    ki_f = ki.astype(jnp.float32)

    def block(args):
        qb, qib, wib, qpb = args
        rel = jax.nn.relu(jnp.einsum('bqhd,bsd->bqhs', qib.astype(jnp.float32), ki_f))
        score = jnp.einsum('bqhs,bqh->bqs', rel, wib.astype(jnp.float32)) * IDX_DIM ** -0.5
        allowed = k_chunk[None, :] <= (qpb // CHUNK)[:, None]
        score = jnp.where(allowed[None], score, -jnp.inf)
        top_val, top_idx = lax.top_k(score, n_sel)
        valid = jnp.isfinite(top_val)
        kg = jax.vmap(lambda kb, ib: kb[ib])(k, top_idx)
        vg = jax.vmap(lambda vb, ib: vb[ib])(v, top_idx)
        dist = jnp.abs(qpb.astype(jnp.float32)[None, :, None] - kpos_f[top_idx])
        qh = qb.reshape(B, qblk, N_KV_HEADS, GROUP, HEAD_DIM)
        logits = jnp.einsum('bqkgd,bqnkd->bqkgn', qh, kg).astype(jnp.float32) * HEAD_DIM ** -0.5
        logits = logits - slopes[None, None, :, :, None] * dist[:, :, None, None, :]
        logits = jnp.where(valid[:, :, None, None, :], logits, -jnp.inf)
        prob = jax.nn.softmax(logits, axis=-1).astype(vg.dtype)
        out = jnp.einsum('bqkgn,bqnkd->bqkgd', prob, vg)
        return out.reshape(B, qblk, N_HEADS * HEAD_DIM)

    def to_blocks(t):
        return jnp.moveaxis(t.reshape((B, nb, qblk) + t.shape[2:]), 1, 0)

    out = lax.map(block, (to_blocks(q), to_blocks(qi), to_blocks(wi), q_pos.reshape(nb, qblk)))
    return jnp.moveaxis(out, 0, 1).reshape(B, T, N_HEADS * HEAD_DIM)


def _layer(x, p, pos, past_kv, conv_buf, h0, wts):
    (w_in, conv_w, conv_b, ga_w, ga_b, gx_w, gx_b, lam, w_out, ln_g, ln_b, ple_proj, ple_gate) = wts
    B, T, _ = x.shape
    u, g_lru, q, k, v, g_att, qi, ki, wi = _split_cols(x @ w_in)
    u, conv_new = _causal_conv(u, conv_buf, conv_w, conv_b)
    h_seq, h_last = _rg_lru(u, pos, h0, ga_w, ga_b, gx_w, gx_b, lam)
    y_lru = h_seq * jax.nn.silu(g_lru)
    q = q.reshape(B, T, N_HEADS, HEAD_DIM)
    k = k.reshape(B, T, N_KV_HEADS, HEAD_DIM)
    v = v.reshape(B, T, N_KV_HEADS, HEAD_DIM)
    qi = qi.reshape(B, T, N_IDX_HEADS, IDX_DIM)
    wi = wi * N_IDX_HEADS ** -0.5
    if past_kv is None:
        k_all, v_all, ki_all = k, v, ki
    else:
        pk, pv, pki = past_kv
        k_all = jnp.concatenate([pk.astype(k.dtype), k], axis=1)
        v_all = jnp.concatenate([pv.astype(v.dtype), v], axis=1)
        ki_all = jnp.concatenate([pki.astype(ki.dtype), ki], axis=1)
    n_keys = k_all.shape[1]
    n_sel = min(TOPK_MAX, n_keys // 4)
    k_pos = jnp.arange(n_keys, dtype=jnp.int32)
    y_att = _indexed_sparse_attention(q, qi, wi, pos, k_all, v_all, ki_all, k_pos, n_sel)
    y_att = y_att * jax.nn.silu(g_att)
    mix = jnp.concatenate([y_lru, y_att], axis=-1) @ w_out
    ple = jax.nn.sigmoid(x @ ple_gate) * (p @ ple_proj)
    y = _layer_norm(DN_ALPHA * x + mix + ple, ln_g, ln_b)
    return y, (k, v, ki, conv_new, h_last)


def setup_inputs(seed: int = 0) -> dict:
    key = jax.random.key(seed)
    ks = jax.random.split(key, 24)
    f32 = jnp.float32

    def nrm(k, shape, scale):
        return jax.random.normal(k, shape, f32) * scale

    a_pow = jax.random.uniform(ks[16], (DEPTH, LRU_W), f32, 0.9, 0.999)
    a_base = a_pow ** (1.0 / LRU_C)
    return {
        'x_prompt': nrm(ks[0], (BATCH, SEQ, D_MODEL), 1.0),
        'x_sample': nrm(ks[1], (DEC_BATCH, DEC_SEQ, D_MODEL), 1.0),
        'cache_k': nrm(ks[2], (DEPTH, DEC_BATCH, PAST_LEN, N_KV_HEADS, HEAD_DIM), 1.0),
        'cache_v': nrm(ks[3], (DEPTH, DEC_BATCH, PAST_LEN, N_KV_HEADS, HEAD_DIM), 1.0),
        'cache_kidx': nrm(ks[4], (DEPTH, DEC_BATCH, PAST_LEN, IDX_DIM), 1.0),
        'state_conv': nrm(ks[5], (DEPTH, DEC_BATCH, CONV_W - 1, LRU_W), 1.0),
        'state_h': nrm(ks[6], (DEPTH, DEC_BATCH, LRU_W), 0.5),
        'p_prompt': nrm(ks[7], (DEPTH, BATCH, SEQ, PLE_DIM), 1.0),
        'p_sample': nrm(ks[8], (DEPTH, DEC_BATCH, DEC_SEQ, PLE_DIM), 1.0),
        'w_in': nrm(ks[9], (DEPTH, D_MODEL, N_IN), D_MODEL ** -0.5),
        'conv_w': nrm(ks[10], (DEPTH, CONV_W, LRU_W), CONV_W ** -0.5),
        'conv_b': nrm(ks[11], (DEPTH, LRU_W), 0.01),
        'gate_a_w': nrm(ks[12], (DEPTH, N_LRU_BLOCKS, LRU_BLOCK, LRU_BLOCK), LRU_BLOCK ** -0.5),
        'gate_a_b': nrm(ks[13], (DEPTH, LRU_W), 0.01),
        'gate_x_w': nrm(ks[14], (DEPTH, N_LRU_BLOCKS, LRU_BLOCK, LRU_BLOCK), LRU_BLOCK ** -0.5),
        'gate_x_b': nrm(ks[15], (DEPTH, LRU_W), 0.01),
        'lru_lambda': jnp.log(a_base) - jnp.log1p(-a_base),
        'w_out': nrm(ks[17], (DEPTH, MIX_W, D_MODEL), MIX_W ** -0.5 * DN_BETA),
        'ln_g': 1.0 + nrm(ks[18], (DEPTH, D_MODEL), 0.02),
        'ln_b': nrm(ks[19], (DEPTH, D_MODEL), 0.02),
        'ple_proj': nrm(ks[20], (DEPTH, PLE_DIM, D_MODEL), PLE_DIM ** -0.5),
        'ple_gate': nrm(ks[21], (DEPTH, D_MODEL, D_MODEL), D_MODEL ** -0.5),
    }


def reference(x_prompt, x_sample, cache_k, cache_v, cache_kidx, state_conv, state_h,
              p_prompt, p_sample, w_in, conv_w, conv_b, gate_a_w, gate_a_b, gate_x_w,
              gate_x_b, lru_lambda, w_out, ln_g, ln_b, ple_proj, ple_gate):
    B, T = x_prompt.shape[:2]
    Ts = x_sample.shape[1]
    past_len = cache_k.shape[2]
    pos_p = jnp.arange(T, dtype=jnp.int32)
    pos_s = past_len + jnp.arange(Ts, dtype=jnp.int32)
    conv0 = jnp.zeros((B, CONV_W - 1, LRU_W), x_prompt.dtype)
    h_zero = jnp.zeros((B, LRU_W), x_prompt.dtype)
    xp, xs = x_prompt, x_sample
    st_p, st_s = [], []
    for i in range(DEPTH):
        wts = (w_in[i], conv_w[i], conv_b[i], gate_a_w[i], gate_a_b[i], gate_x_w[i], gate_x_b[i],
               lru_lambda[i], w_out[i], ln_g[i], ln_b[i], ple_proj[i], ple_gate[i])
        xp, sp = _layer(xp, p_prompt[i], pos_p, None, conv0, h_zero, wts)
        xs, ss = _layer(xs, p_sample[i], pos_s, (cache_k[i], cache_v[i], cache_kidx[i]),
                        state_conv[i], state_h[i], wts)
        st_p.append(sp)
        st_s.append(ss)
    k_p = jnp.stack([s[0] for s in st_p])
    v_p = jnp.stack([s[1] for s in st_p])
    ki_p = jnp.stack([s[2] for s in st_p])
    conv_p = jnp.stack([s[3] for s in st_p])
    h_p = jnp.stack([s[4] for s in st_p])
    k_s = jnp.stack([s[0] for s in st_s])
    v_s = jnp.stack([s[1] for s in st_s])
    ki_s = jnp.stack([s[2] for s in st_s])
    conv_s = jnp.stack([s[3] for s in st_s])
    h_s = jnp.stack([s[4] for s in st_s])
    return (xp, xs, k_p, v_p, ki_p, conv_p, h_p, k_s, v_s, ki_s, conv_s, h_s)
```

```python
import functools

import jax
import jax.numpy as jnp
from jax import lax
from jax.experimental import pallas as pl
from jax.experimental.pallas import tpu as pltpu

CHUNK = 64
CHUNK_SHIFT = 6
N_LRU_BLOCKS = 16
CONV_W = 4
LRU_C = 8.0
HEAD_DIM = 128
N_KV_HEADS = 4
N_IDX_HEADS = 32
IDX_DIM = 128
TOPK_MAX = 256
LN_EPS = 1e-5

LANES = 128
SUBLANES = 8
VMEM_LIMIT_BYTES = 56 * 1024 * 1024

INT_MIN = -(2 ** 31)
KEY_NEG_INF = -2139095041

_NT = (((1,), (1,)), ((), ()))


def _pick_tile(n, candidates):
    for c in candidates:
        if n % c == 0:
            return c
    return n


def _params(sem):
    return pltpu.CompilerParams(dimension_semantics=sem, vmem_limit_bytes=VMEM_LIMIT_BYTES)


def _mm_kernel(x_ref, w_ref, o_ref):
    o_ref[...] = jnp.dot(x_ref[...], w_ref[...], preferred_element_type=jnp.float32)


def _matmul(x, w, tm, tn):
    m, k = x.shape
    n = w.shape[1]
    return pl.pallas_call(
        _mm_kernel,
        grid=(m // tm, n // tn),
        in_specs=[pl.BlockSpec((tm, k), lambda i, j: (i, 0)),
                  pl.BlockSpec((k, tn), lambda i, j: (0, j))],
        out_specs=pl.BlockSpec((tm, tn), lambda i, j: (i, j)),
        out_shape=jax.ShapeDtypeStruct((m, n), jnp.float32),
        compiler_params=_params(("parallel", "parallel")),
        name="in_proj",
    )(x, w)


def _lru_kernel(u_ref, g_ref, cbuf_ref, h0_ref, cw_ref, cb_ref, gaw_ref, gab_ref,
                gxw_ref, gxb_ref, lam_ref, y_ref, cnew_ref, hlast_ref,
                full_ref, conv_ref, a_ref, b_ref, *, t_len, ct, rc, pos0):
    full_ref[0:SUBLANES, :] = cbuf_ref[...]
    full_ref[SUBLANES:SUBLANES + t_len, :] = u_ref[...]
    n_tiles = t_len // SUBLANES
    row = lax.broadcasted_iota(jnp.int32, (SUBLANES, ct), 0)
    cw = cw_ref[...]
    cb = cb_ref[...]

    def conv_tile(i, carry):
        off = pl.multiple_of(i * SUBLANES, SUBLANES)
        prev = full_ref[pl.ds(off, SUBLANES), :]
        cur = full_ref[pl.ds(off + SUBLANES, SUBLANES), :]
        acc = cb + cur * cw[CONV_W - 1:CONV_W, :]
        for s in range(1, CONV_W):
            shifted = jnp.where(row >= s, pltpu.roll(cur, s, 0), pltpu.roll(prev, s, 0))
            acc = acc + shifted * cw[CONV_W - 1 - s:CONV_W - s, :]
        conv_ref[pl.ds(off, SUBLANES), :] = acc
        return carry

    lax.fori_loop(0, n_tiles, conv_tile, 0)
    cnew_ref[...] = pltpu.roll(full_ref[t_len:t_len + SUBLANES, :], CONV_W - 1, 0)

    nlam = -lam_ref[...]
    softplus = jnp.maximum(nlam, 0.0) + jnp.log1p(jnp.exp(-jnp.abs(nlam)))
    gab = gab_ref[...]
    gxb = gxb_ref[...]

    def gate_chunk(c, carry):
        r0 = pl.multiple_of(c * rc, rc)
        conv = conv_ref[pl.ds(r0, rc), :]
        conv16 = conv.astype(jnp.bfloat16)
        r_parts, i_parts = [], []
        for n in range(ct // LANES):
            ub = conv16[:, n * LANES:(n + 1) * LANES]
            r_parts.append(jnp.dot(ub, gaw_ref[n], preferred_element_type=jnp.float32))
            i_parts.append(jnp.dot(ub, gxw_ref[n], preferred_element_type=jnp.float32))
        r = jax.nn.sigmoid(jnp.concatenate(r_parts, axis=1) + gab)
        gi = jax.nn.sigmoid(jnp.concatenate(i_parts, axis=1) + gxb)
        log_a = -LRU_C * r * softplus
        th = jnp.tanh(log_a)
        mult = jnp.sqrt(-2.0 * th / (1.0 - th))
        pos = pos0 + r0 + lax.broadcasted_iota(jnp.int32, (rc, ct), 0)
        mult = jnp.where(pos == 0, 1.0, mult)
        a_ref[pl.ds(r0, rc), :] = jnp.exp(log_a)
        b_ref[pl.ds(r0, rc), :] = conv * gi * mult
        return carry

    lax.fori_loop(0, t_len // rc, gate_chunk, 0)

    def scan_tile(i, h):
        off = pl.multiple_of(i * SUBLANES, SUBLANES)
        ac = a_ref[pl.ds(off, SUBLANES), :]
        bc = b_ref[pl.ds(off, SUBLANES), :]
        for s in (1, 2, 4):
            a_sh = jnp.where(row >= s, pltpu.roll(ac, s, 0), 1.0)
            b_sh = jnp.where(row >= s, pltpu.roll(bc, s, 0), 0.0)
            bc = ac * b_sh + bc
            ac = ac * a_sh
        hc = ac * h + bc
        b_ref[pl.ds(off, SUBLANES), :] = hc
        return hc[SUBLANES - 1:SUBLANES, :]

    hlast_ref[...] = lax.fori_loop(0, n_tiles, scan_tile, h0_ref[...])

    def out_chunk(c, carry):
        r0 = pl.multiple_of(c * rc, rc)
        g = g_ref[pl.ds(r0, rc), :]
        y_ref[pl.ds(r0, rc), :] = (b_ref[pl.ds(r0, rc), :] * (g * jax.nn.sigmoid(g))).astype(y_ref.dtype)
        return carry

    lax.fori_loop(0, t_len // rc, out_chunk, 0)


def _lru_call(z, row_blk0, n_b, t_len, ct, u_col, g_col, cbuf, h0, cw, cb, gaw, gab, gxw, gxb, lam, pos0):
    lru_w = cw.shape[1]
    n_c = lru_w // ct
    nb_blk = ct // LANES
    rc = _pick_tile(t_len, (256, 128, 64, 32, 16))
    kern = functools.partial(_lru_kernel, t_len=t_len, ct=ct, rc=rc, pos0=pos0)
    vec = lambda: pl.BlockSpec((1, ct), lambda b, c: (0, c))
    return pl.pallas_call(
        kern,
        grid=(n_b, n_c),
        in_specs=[
            pl.BlockSpec((t_len, ct), lambda b, c: (row_blk0 + b, u_col // ct + c)),
            pl.BlockSpec((t_len, ct), lambda b, c: (row_blk0 + b, g_col // ct + c)),
            pl.BlockSpec((None, SUBLANES, ct), lambda b, c: (b, 0, c)),
            pl.BlockSpec((None, 1, ct), lambda b, c: (b, 0, c)),
            pl.BlockSpec((CONV_W, ct), lambda b, c: (0, c)),
            vec(),
            pl.BlockSpec((nb_blk, LANES, LANES), lambda b, c: (c, 0, 0)),
            vec(),
            pl.BlockSpec((nb_blk, LANES, LANES), lambda b, c: (c, 0, 0)),
            vec(),
            vec(),
        ],
        out_specs=[
            pl.BlockSpec((t_len, ct), lambda b, c: (b, c)),
            pl.BlockSpec((None, SUBLANES, ct), lambda b, c: (b, 0, c)),
            pl.BlockSpec((None, 1, ct), lambda b, c: (b, 0, c)),
        ],
        out_shape=[
            jax.ShapeDtypeStruct((n_b * t_len, lru_w), jnp.bfloat16),
            jax.ShapeDtypeStruct((n_b, SUBLANES, lru_w), jnp.float32),
            jax.ShapeDtypeStruct((n_b, 1, lru_w), jnp.float32),
        ],
        scratch_shapes=[
            pltpu.VMEM((t_len + SUBLANES, ct), jnp.float32),
            pltpu.VMEM((t_len, ct), jnp.float32),
            pltpu.VMEM((t_len, ct), jnp.float32),
            pltpu.VMEM((t_len, ct), jnp.float32),
        ],
        compiler_params=_params(("parallel", "parallel")),
        name="rg_lru",
    )(z, z, cbuf, h0, cw, cb, gaw, gab, gxw, gxb, lam)


def _attn_kernel(*refs, tqv, n_past, t_cur, pos0, n_sel, kc, n_heads):
    has_past = n_past > 0
    if has_past:
        (qi_ref, q_ref, ga_ref, wt_ref, kcur_ref, vcur_ref, kicur_ref,
         kp_ref, vp_ref, kip_ref, y_ref,
         kb_ref, vt_ref, kib_ref, qis_ref, qs_ref, key_ref, dm_ref, l_ref, pt_ref, stage_ref) = refs
    else:
        (qi_ref, q_ref, ga_ref, wt_ref, kcur_ref, vcur_ref, kicur_ref, y_ref,
         kb_ref, vt_ref, kib_ref, qis_ref, qs_ref, key_ref, dm_ref, l_ref, pt_ref, stage_ref) = refs
    tq = LANES
    group = n_heads // N_KV_HEADS
    s_all = n_past + t_cur
    s_pad = key_ref.shape[0]
    b = pl.program_id(0)
    qb = pl.program_id(1)

    def load_block(k_src, v_src, ki_src, src_row, dst_row):
        rows = slice(src_row, src_row + LANES)
        kib_ref[dst_row:dst_row + LANES, :] = ki_src[rows, 0:IDX_DIM].astype(jnp.bfloat16)
        for g in range(N_KV_HEADS):
            cols = slice(g * HEAD_DIM, (g + 1) * HEAD_DIM)
            kb_ref[g, dst_row:dst_row + LANES, :] = k_src[rows, cols].astype(jnp.bfloat16)
            vt_ref[g, :, dst_row:dst_row + LANES] = v_src[rows, cols].T.astype(jnp.bfloat16)

    @pl.when(qb == 0)
    def _assemble_keys():
        if has_past:
            for blk in range(n_past // LANES):
                load_block(kp_ref, vp_ref, kip_ref, blk * LANES, blk * LANES)
        if t_cur % LANES == 0:
            for blk in range(t_cur // LANES):
                load_block(kcur_ref, vcur_ref, kicur_ref, blk * LANES, n_past + blk * LANES)
        else:
            kv_w = N_KV_HEADS * HEAD_DIM
            stage_ref[...] = jnp.zeros(stage_ref.shape, jnp.float32)
            stage_ref[0, 0:t_cur, 0:kv_w] = kcur_ref[...]
            stage_ref[1, 0:t_cur, 0:kv_w] = vcur_ref[...]
            stage_ref[2, 0:t_cur, 0:IDX_DIM] = kicur_ref[...]
            load_block(stage_ref.at[0], stage_ref.at[1], stage_ref.at[2], 0, n_past)

    if tqv < tq:
        @pl.when((b == 0) & (qb == 0))
        def _zero_query_padding():
            qis_ref[...] = jnp.zeros(qis_ref.shape, qis_ref.dtype)
            qs_ref[...] = jnp.zeros(qs_ref.shape, qs_ref.dtype)

    for h in range(N_IDX_HEADS):
        qis_ref[h * tq:h * tq + tqv, :] = qi_ref[:, h * IDX_DIM:(h + 1) * IDX_DIM].astype(jnp.bfloat16)
    for h in range(n_heads):
        qs_ref[h * tq:h * tq + tqv, :] = q_ref[:, h * HEAD_DIM:(h + 1) * HEAD_DIM].astype(jnp.bfloat16)

    w = wt_ref[...] * (N_IDX_HEADS ** -0.5)
    qpos = pos0 + qb * tqv + lax.broadcasted_iota(jnp.int32, (1, tq), 1)
    qchunk = lax.shift_right_logical(qpos, CHUNK_SHIFT)

    n_main = s_pad // kc
    tail = s_pad - n_main * kc

    def for_chunks(fn, carry):
        def body(c, cr):
            return fn(pl.multiple_of(c * kc, kc), kc, cr)
        carry = lax.fori_loop(0, n_main, body, carry)
        if tail:
            carry = fn(n_main * kc, tail, carry)
        return carry

    def score_chunk(r0, n, carry):
        kic = kib_ref[pl.ds(r0, n), :]
        sc = jnp.zeros((n, tq), jnp.float32)
        for hg in range(N_IDX_HEADS // 4):
            rel = lax.dot_general(kic, qis_ref[hg * 4 * tq:(hg + 1) * 4 * tq, :], _NT,
                                  preferred_element_type=jnp.float32)
            for j in range(4):
                h = hg * 4 + j
                sc = sc + jnp.maximum(rel[:, j * tq:(j + 1) * tq], 0.0) * w[h:h + 1, :]
        sc = sc * (IDX_DIM ** -0.5)
        kpos = r0 + lax.broadcasted_iota(jnp.int32, (n, tq), 0)
        allowed = (lax.shift_right_logical(kpos, CHUNK_SHIFT) <= qchunk) & (kpos < s_all)
        sc = jnp.where(allowed, sc, -jnp.inf)
        bits = pltpu.bitcast(sc, jnp.int32)
        key_ref[pl.ds(r0, n), :] = jnp.where(bits < 0, bits ^ 0x7FFFFFFF, bits)
        return carry

    for_chunks(score_chunk, 0)

    def count_ge(cand):
        return jnp.sum(jnp.where(key_ref[...] >= cand, 1.0, 0.0), axis=0, keepdims=True)

    thr = jnp.where(count_ge(jnp.zeros((1, tq), jnp.int32)) >= n_sel, 0, INT_MIN).astype(jnp.int32)

    def bit_body(i, t):
        cand = t + lax.shift_left(jnp.int32(1), 30 - i)
        return jnp.where(count_ge(cand) >= n_sel, cand, t)

    thr = lax.fori_loop(0, 31, bit_body, thr)
    thr = jnp.maximum(thr, KEY_NEG_INF + 1)

    kpos_all = lax.broadcasted_iota(jnp.int32, (s_pad, tq), 0)
    dist = jnp.abs(qpos - kpos_all).astype(jnp.float32)
    dm_ref[...] = jnp.where(key_ref[...] >= thr, dist, jnp.inf)

    scale = HEAD_DIM ** -0.5
    for g in range(N_KV_HEADS):
        heads = [g * group + j for j in range(group)]
        slopes = [2.0 ** (-8.0 * (h + 1) / n_heads) for h in heads]

        def logit_chunk(r0, n, m, g=g, slopes=slopes):
            qg = qs_ref[g * group * tq:(g + 1) * group * tq, :]
            lg = lax.dot_general(kb_ref[g, pl.ds(r0, n), :], qg, _NT,
                                 preferred_element_type=jnp.float32)
            dmc = dm_ref[pl.ds(r0, n), :]
            new_m = []
            for j in range(group):
                lj = lg[:, j * tq:(j + 1) * tq] * scale - slopes[j] * dmc
                l_ref[pl.ds(r0, n), j * tq:(j + 1) * tq] = lj
                new_m.append(jnp.maximum(m[j], jnp.max(lj, axis=0, keepdims=True)))
            return tuple(new_m)

        m = for_chunks(logit_chunk, tuple(jnp.full((1, tq), -jnp.inf, jnp.float32) for _ in range(group)))

        def prob_chunk(r0, n, s, m=m):
            new_s = []
            for j in range(group):
                p = jnp.exp(l_ref[pl.ds(r0, n), j * tq:(j + 1) * tq] - m[j])
                pt_ref[pl.ds(r0, n), j * tq:(j + 1) * tq] = p.astype(jnp.bfloat16)
                new_s.append(s[j] + jnp.sum(p, axis=0, keepdims=True))
            return tuple(new_s)

        s = for_chunks(prob_chunk, tuple(jnp.zeros((1, tq), jnp.float32) for _ in range(group)))

        ot = jnp.dot(vt_ref[g], pt_ref[...], preferred_element_type=jnp.float32)
        for j in range(group):
            h = heads[j]
            o = (ot[:, j * tq:(j + 1) * tq] / s[j]).T
            gate = ga_ref[:, h * HEAD_DIM:(h + 1) * HEAD_DIM]
            y_ref[:, h * HEAD_DIM:(h + 1) * HEAD_DIM] = (
                o[0:tqv, :] * (gate * jax.nn.sigmoid(gate))).astype(y_ref.dtype)


def _attn_call(z, ki, wt, past, *, row_blk0, n_b, n_q, tqv, t_cur, cur_blk0, pos0, cols):
    qi_col, q_col, ga_col, k_col, v_col = cols
    n_heads = (ga_col - q_col) // HEAD_DIM
    att_w = n_heads * HEAD_DIM
    qi_w = N_IDX_HEADS * IDX_DIM
    kv_w = N_KV_HEADS * HEAD_DIM
    n_past = 0 if past is None else past[0].shape[1]
    s_all = n_past + t_cur
    s_pad = n_past + ((t_cur + LANES - 1) // LANES) * LANES
    n_sel = min(TOPK_MAX, s_all // 4)
    kc = _pick_tile(s_pad, (512, 256, 128)) if s_pad <= 512 else 512
    kern = functools.partial(_attn_kernel, tqv=tqv, n_past=n_past, t_cur=t_cur, pos0=pos0,
                             n_sel=n_sel, kc=kc, n_heads=n_heads)
    qrow = lambda b, q: row_blk0 + b * n_q + q
    in_specs = [
        pl.BlockSpec((tqv, qi_w), lambda b, q: (qrow(b, q), qi_col // qi_w)),
        pl.BlockSpec((tqv, att_w), lambda b, q: (qrow(b, q), q_col // att_w)),
        pl.BlockSpec((tqv, att_w), lambda b, q: (qrow(b, q), ga_col // att_w)),
        pl.BlockSpec((N_IDX_HEADS, LANES), lambda b, q: (0, b * n_q + q)),
        pl.BlockSpec((t_cur, kv_w), lambda b, q: (cur_blk0 + b, k_col // kv_w)),
        pl.BlockSpec((t_cur, kv_w), lambda b, q: (cur_blk0 + b, v_col // kv_w)),
        pl.BlockSpec((t_cur, IDX_DIM), lambda b, q: (cur_blk0 + b, 0)),
    ]
    args = [z, z, z, wt, z, z, ki]
    if past is not None:
        in_specs += [
            pl.BlockSpec((None, n_past, kv_w), lambda b, q: (b, 0, 0)),
            pl.BlockSpec((None, n_past, kv_w), lambda b, q: (b, 0, 0)),
            pl.BlockSpec((None, n_past, IDX_DIM), lambda b, q: (b, 0, 0)),
        ]
        args += list(past)
    return pl.pallas_call(
        kern,
        grid=(n_b, n_q),
        in_specs=in_specs,
        out_specs=pl.BlockSpec((tqv, att_w), lambda b, q: (b * n_q + q, 0)),
        out_shape=jax.ShapeDtypeStruct((n_b * n_q * tqv, att_w), jnp.bfloat16),
        scratch_shapes=[
            pltpu.VMEM((N_KV_HEADS, s_pad, HEAD_DIM), jnp.bfloat16),
            pltpu.VMEM((N_KV_HEADS, HEAD_DIM, s_pad), jnp.bfloat16),
            pltpu.VMEM((s_pad, IDX_DIM), jnp.bfloat16),
            pltpu.VMEM((N_IDX_HEADS * LANES, IDX_DIM), jnp.bfloat16),
            pltpu.VMEM((n_heads * LANES, HEAD_DIM), jnp.bfloat16),
            pltpu.VMEM((s_pad, LANES), jnp.int32),
            pltpu.VMEM((s_pad, LANES), jnp.float32),
            pltpu.VMEM((s_pad, (n_heads // N_KV_HEADS) * LANES), jnp.float32),
            pltpu.VMEM((s_pad, (n_heads // N_KV_HEADS) * LANES), jnp.bfloat16),
            pltpu.VMEM((3, LANES, kv_w), jnp.float32),
        ],
        compiler_params=_params(("arbitrary", "arbitrary")),
        name="sparse_attn",
    )(*args)


def _out_kernel(yl_ref, ya_ref, xb_ref, p_ref, xf_ref, wo_ref, wg_ref, wp_ref, o_ref, *, alpha):
    half = yl_ref.shape[1]
    mix = (jnp.dot(yl_ref[...], wo_ref[0:half, :], preferred_element_type=jnp.float32)
           + jnp.dot(ya_ref[...], wo_ref[half:, :], preferred_element_type=jnp.float32))
    gate = jax.nn.sigmoid(jnp.dot(xb_ref[...], wg_ref[...], preferred_element_type=jnp.float32))
    emb = jnp.dot(p_ref[...], wp_ref[...], preferred_element_type=jnp.float32)
    o_ref[...] = alpha * xf_ref[...] + mix + gate * emb


def _out_call(yl, ya, xb, p, xf, wo, wg, wp, alpha, tm, tn):
    m, d = xf.shape
    half = yl.shape[1]
    ple = p.shape[1]
    row = lambda w: pl.BlockSpec((tm, w), lambda i, j: (i, 0))
    col = lambda k: pl.BlockSpec((k, tn), lambda i, j: (0, j))
    return pl.pallas_call(
        functools.partial(_out_kernel, alpha=alpha),
        grid=(m // tm, d // tn),
        in_specs=[row(half), row(ya.shape[1]), row(d), row(ple),
                  pl.BlockSpec((tm, tn), lambda i, j: (i, j)),
                  col(wo.shape[0]), col(d), col(ple)],
        out_specs=pl.BlockSpec((tm, tn), lambda i, j: (i, j)),
        out_shape=jax.ShapeDtypeStruct((m, d), jnp.float32),
        compiler_params=_params(("parallel", "parallel")),
        name="out_proj",
    )(yl, ya, xb, p, xf, wo, wg, wp)


def _ln_kernel(x_ref, g_ref, b_ref, of_ref, ob_ref, *, rs):
    def sub_block(c, carry):
        rows = pl.ds(pl.multiple_of(c * rs, rs), rs)
        x = x_ref[rows, :]
        mu = jnp.mean(x, axis=-1, keepdims=True)
        xc = x - mu
        var = jnp.mean(xc * xc, axis=-1, keepdims=True)
        y = xc * lax.rsqrt(var + LN_EPS) * g_ref[...] + b_ref[...]
        of_ref[rows, :] = y
        ob_ref[rows, :] = y.astype(ob_ref.dtype)
        return carry

    lax.fori_loop(0, x_ref.shape[0] // rs, sub_block, 0)


def _ln_call(x, g, b, tr):
    m, d = x.shape
    return pl.pallas_call(
        functools.partial(_ln_kernel, rs=_pick_tile(tr, (32, 16))),
        grid=(m // tr,),
        in_specs=[pl.BlockSpec((tr, d), lambda i: (i, 0)),
                  pl.BlockSpec((1, d), lambda i: (0, 0)),
                  pl.BlockSpec((1, d), lambda i: (0, 0))],
        out_specs=[pl.BlockSpec((tr, d), lambda i: (i, 0)),
                   pl.BlockSpec((tr, d), lambda i: (i, 0))],
        out_shape=[jax.ShapeDtypeStruct((m, d), jnp.float32),
                   jax.ShapeDtypeStruct((m, d), jnp.bfloat16)],
        compiler_params=_params(("parallel",)),
        name="layer_norm",
    )(x, g, b)


def kernel(x_prompt, x_sample, cache_k, cache_v, cache_kidx, state_conv, state_h, p_prompt, p_sample, w_in, conv_w, conv_b, gate_a_w, gate_a_b, gate_x_w, gate_x_b, lru_lambda, w_out, ln_g, ln_b, ple_proj, ple_gate):
    bf16 = jnp.bfloat16
    n_bp, t_p, d = x_prompt.shape
    n_bs, t_s, _ = x_sample.shape
    depth = w_in.shape[0]
    n_past = cache_k.shape[2]
    lru_w = conv_w.shape[2]
    att_w = d - lru_w
    n_heads = att_w // HEAD_DIM
    kv_w = N_KV_HEADS * HEAD_DIM
    qi_w = N_IDX_HEADS * IDX_DIM
    alpha = (2 * depth) ** 0.25
    m_p, m_s = n_bp * t_p, n_bs * t_s
    m = m_p + m_s
    assert t_p % LANES == 0 and m_p % t_s == 0 and t_s % SUBLANES == 0 and n_past % LANES == 0
    assert lru_w // N_LRU_BLOCKS == LANES and CHUNK == 1 << CHUNK_SHIFT

    o_u, o_gl = 0, lru_w
    o_q = 2 * lru_w
    o_k = o_q + att_w
    o_v = o_k + kv_w
    o_ga = o_v + kv_w
    o_qi = o_ga + att_w
    o_ki = o_qi + qi_w
    c_qi, c_q = 0, qi_w
    c_ga = c_q + att_w
    c_u = c_ga + att_w
    c_gl = c_u + lru_w
    c_k = c_gl + lru_w
    c_v = c_k + kv_w
    n_main = c_v + kv_w

    tm = _pick_tile(m, (768, 512, 384, 256, 128))
    tn_in = _pick_tile(n_main, (1024, 512))
    tn_out = _pick_tile(d, (512,))
    tr_ln = _pick_tile(m, (256, 128))
    ct_p = _pick_tile(lru_w, (256,))
    tq = LANES
    n_q = t_p // tq

    xf = jnp.concatenate([x_prompt.reshape(m_p, d), x_sample.reshape(m_s, d)], axis=0)
    xb = xf.astype(bf16)
    hist_pad = ((0, 0), (SUBLANES - (CONV_W - 1), 0), (0, 0))
    zero_conv = jnp.zeros((n_bp, SUBLANES, lru_w), jnp.float32)
    zero_h = jnp.zeros((n_bp, 1, lru_w), jnp.float32)

    outs_p, outs_s = [], []
    for i in range(depth):
        wl = w_in[i]
        seg = lambda o, n: wl[:, o:o + n]
        w_main = jnp.concatenate(
            [seg(o_qi, qi_w), seg(o_q, att_w), seg(o_ga, att_w), seg(o_u, lru_w), seg(o_gl, lru_w),
             seg(o_k, kv_w), seg(o_v, kv_w)], axis=1).astype(bf16)
        w_side = wl[:, o_ki:].astype(bf16)

        z = _matmul(xb, w_main, tm, tn_in)
        zs = _matmul(xb, w_side, tm, w_side.shape[1])
        ki = zs[:, :IDX_DIM]
        wi = zs[:, IDX_DIM:]
        wt_p = wi[:m_p].T
        wt_s = jnp.pad(wi[m_p:].reshape(n_bs, t_s, N_IDX_HEADS),
                       ((0, 0), (0, LANES - t_s), (0, 0))).reshape(n_bs * LANES, N_IDX_HEADS).T

        lru_args = (conv_w[i], conv_b[i][None], gate_a_w[i].astype(bf16), gate_a_b[i][None],
                    gate_x_w[i].astype(bf16), gate_x_b[i][None], lru_lambda[i][None])
        yl_p, conv_p, h_p = _lru_call(z, 0, n_bp, t_p, ct_p, c_u, c_gl, zero_conv, zero_h, *lru_args, pos0=0)
        yl_s, conv_s, h_s = _lru_call(z, m_p // t_s, n_bs, t_s, lru_w, c_u, c_gl,
                                      jnp.pad(state_conv[i], hist_pad),
                                      state_h[i][:, None, :], *lru_args, pos0=n_past)
        conv_p, conv_s = conv_p[:, :CONV_W - 1], conv_s[:, :CONV_W - 1]

        cols = (c_qi, c_q, c_ga, c_k, c_v)
        ya_p = _attn_call(z, ki, wt_p, None, row_blk0=0, n_b=n_bp, n_q=n_q, tqv=tq, t_cur=t_p,
                          cur_blk0=0, pos0=0, cols=cols)
        past = (cache_k[i].reshape(n_bs, n_past, kv_w), cache_v[i].reshape(n_bs, n_past, kv_w), cache_kidx[i])
        ya_s = _attn_call(z, ki, wt_s, past, row_blk0=m_p // t_s, n_b=n_bs, n_q=1, tqv=t_s, t_cur=t_s,
                          cur_blk0=m_p // t_s, pos0=n_past, cols=cols)

        yl = jnp.concatenate([yl_p, yl_s], axis=0)
        ya = jnp.concatenate([ya_p, ya_s], axis=0)
        p = jnp.concatenate([p_prompt[i].reshape(m_p, -1), p_sample[i].reshape(m_s, -1)], axis=0).astype(bf16)
        pre = _out_call(yl, ya, xb, p, xf, w_out[i].astype(bf16), ple_gate[i].astype(bf16),
                        ple_proj[i].astype(bf16), alpha, tm, tn_out)
        xf, xb = _ln_call(pre, ln_g[i][None], ln_b[i][None], tr_ln)

        outs_p.append((z[:m_p, c_k:c_k + kv_w].reshape(n_bp, t_p, N_KV_HEADS, HEAD_DIM),
                       z[:m_p, c_v:c_v + kv_w].reshape(n_bp, t_p, N_KV_HEADS, HEAD_DIM),
                       ki[:m_p].reshape(n_bp, t_p, IDX_DIM), conv_p, h_p[:, 0, :]))
        outs_s.append((z[m_p:, c_k:c_k + kv_w].reshape(n_bs, t_s, N_KV_HEADS, HEAD_DIM),
                       z[m_p:, c_v:c_v + kv_w].reshape(n_bs, t_s, N_KV_HEADS, HEAD_DIM),
                       ki[m_p:].reshape(n_bs, t_s, IDX_DIM), conv_s, h_s[:, 0, :]))

    stack = lambda outs, j: jnp.stack([o[j] for o in outs])
    y_p = xf[:m_p].reshape(n_bp, t_p, d)
    y_s = xf[m_p:].reshape(n_bs, t_s, d)
    return (y_p, y_s,
            stack(outs_p, 0), stack(outs_p, 1), stack(outs_p, 2), stack(outs_p, 3), stack(outs_p, 4),
            stack(outs_s, 0), stack(outs_s, 1), stack(outs_s, 2), stack(outs_s, 3), stack(outs_s, 4))
```

```python
import functools

import jax
import jax.numpy as jnp
from jax import lax
from jax.experimental import pallas as pl
from jax.experimental.pallas import tpu as pltpu

CHUNK = 64
CHUNK_SHIFT = 6
N_LRU_BLOCKS = 16
CONV_W = 4
LRU_C = 8.0
HEAD_DIM = 128
N_KV_HEADS = 4
N_IDX_HEADS = 32
IDX_DIM = 128
TOPK_MAX = 256
LN_EPS = 1e-5
LOG2_E = 1.4426950408889634

LANES = 128
SUBLANES = 8
VMEM_LIMIT_BYTES = 56 * 1024 * 1024

KEY_BLOCK = 256
KEY_BLOCK_SHIFT = 8

INT_MIN = -(2 ** 31)
KEY_NEG_INF = -2139095041

_NT = (((1,), (1,)), ((), ()))


def _pick_tile(n, candidates):
    for c in candidates:
        if n % c == 0:
            return c
    return n


def _params(sem):
    return pltpu.CompilerParams(dimension_semantics=sem, vmem_limit_bytes=VMEM_LIMIT_BYTES)


def _mm_kernel(x_ref, w_ref, o_ref):
    o_ref[...] = jnp.dot(x_ref[...], w_ref[...], preferred_element_type=jnp.float32)


def _matmul(x, w, tm, tn):
    m, k = x.shape
    n = w.shape[1]
    return pl.pallas_call(
        _mm_kernel,
        grid=(m // tm, n // tn),
        in_specs=[pl.BlockSpec((tm, k), lambda i, j: (i, 0)),
                  pl.BlockSpec((k, tn), lambda i, j: (0, j))],
        out_specs=pl.BlockSpec((tm, tn), lambda i, j: (i, j)),
        out_shape=jax.ShapeDtypeStruct((m, n), jnp.float32),
        compiler_params=_params(("parallel", "parallel")),
        name="in_proj",
    )(x, w)


def _lru_kernel(u_ref, g_ref, cbuf_ref, h0_ref, cw_ref, cb_ref, gaw_ref, gab_ref,
                gxw_ref, gxb_ref, lam_ref, y_ref, cnew_ref, hlast_ref,
                full_ref, conv_ref, a_ref, b_ref, *, t_len, ct, rc, pos0):
    full_ref[0:SUBLANES, :] = cbuf_ref[...]
    full_ref[SUBLANES:SUBLANES + t_len, :] = u_ref[...]
    n_tiles = t_len // SUBLANES
    row = lax.broadcasted_iota(jnp.int32, (SUBLANES, ct), 0)
    cw = cw_ref[...]
    cb = cb_ref[...]

    def conv_tile(i, carry):
        off = pl.multiple_of(i * SUBLANES, SUBLANES)
        prev = full_ref[pl.ds(off, SUBLANES), :]
        cur = full_ref[pl.ds(off + SUBLANES, SUBLANES), :]
        acc = cb + cur * cw[CONV_W - 1:CONV_W, :]
        for s in range(1, CONV_W):
            shifted = jnp.where(row >= s, pltpu.roll(cur, s, 0), pltpu.roll(prev, s, 0))
            acc = acc + shifted * cw[CONV_W - 1 - s:CONV_W - s, :]
        conv_ref[pl.ds(off, SUBLANES), :] = acc
        return carry

    lax.fori_loop(0, n_tiles, conv_tile, 0)
    cnew_ref[...] = pltpu.roll(full_ref[t_len:t_len + SUBLANES, :], CONV_W - 1, 0)

    nlam = -lam_ref[...]
    softplus = jnp.maximum(nlam, 0.0) + jnp.log1p(jnp.exp(-jnp.abs(nlam)))
    gab = gab_ref[...]
    gxb = gxb_ref[...]

    def gate_chunk(c, carry):
        r0 = pl.multiple_of(c * rc, rc)
        conv = conv_ref[pl.ds(r0, rc), :]
        conv16 = conv.astype(jnp.bfloat16)
        r_parts, i_parts = [], []
        for n in range(ct // LANES):
            ub = conv16[:, n * LANES:(n + 1) * LANES]
            r_parts.append(jnp.dot(ub, gaw_ref[n], preferred_element_type=jnp.float32))
            i_parts.append(jnp.dot(ub, gxw_ref[n], preferred_element_type=jnp.float32))
        r = jax.nn.sigmoid(jnp.concatenate(r_parts, axis=1) + gab)
        gi = jax.nn.sigmoid(jnp.concatenate(i_parts, axis=1) + gxb)
        log_a = -LRU_C * r * softplus
        th = jnp.tanh(log_a)
        mult = jnp.sqrt(-2.0 * th / (1.0 - th))
        pos = pos0 + r0 + lax.broadcasted_iota(jnp.int32, (rc, ct), 0)
        mult = jnp.where(pos == 0, 1.0, mult)
        a_ref[pl.ds(r0, rc), :] = jnp.exp(log_a)
        b_ref[pl.ds(r0, rc), :] = conv * gi * mult
        return carry

    lax.fori_loop(0, t_len // rc, gate_chunk, 0)

    def scan_tile(i, h):
        off = pl.multiple_of(i * SUBLANES, SUBLANES)
        ac = a_ref[pl.ds(off, SUBLANES), :]
        bc = b_ref[pl.ds(off, SUBLANES), :]
        for s in (1, 2, 4):
            a_sh = jnp.where(row >= s, pltpu.roll(ac, s, 0), 1.0)
            b_sh = jnp.where(row >= s, pltpu.roll(bc, s, 0), 0.0)
            bc = ac * b_sh + bc
            ac = ac * a_sh
        hc = ac * h + bc
        b_ref[pl.ds(off, SUBLANES), :] = hc
        return hc[SUBLANES - 1:SUBLANES, :]

    hlast_ref[...] = lax.fori_loop(0, n_tiles, scan_tile, h0_ref[...])

    def out_chunk(c, carry):
        r0 = pl.multiple_of(c * rc, rc)
        g = g_ref[pl.ds(r0, rc), :]
        y_ref[pl.ds(r0, rc), :] = (b_ref[pl.ds(r0, rc), :] * (g * jax.nn.sigmoid(g))).astype(y_ref.dtype)
        return carry

    lax.fori_loop(0, t_len // rc, out_chunk, 0)


def _lru_call(z, row_blk0, n_b, t_len, ct, u_col, g_col, cbuf, h0, cw, cb, gaw, gab, gxw, gxb, lam, pos0):
    lru_w = cw.shape[1]
    n_c = lru_w // ct
    nb_blk = ct // LANES
    rc = _pick_tile(t_len, (256, 128, 64, 32, 16))
    kern = functools.partial(_lru_kernel, t_len=t_len, ct=ct, rc=rc, pos0=pos0)
    vec = lambda: pl.BlockSpec((1, ct), lambda b, c: (0, c))
    return pl.pallas_call(
        kern,
        grid=(n_b, n_c),
        in_specs=[
            pl.BlockSpec((t_len, ct), lambda b, c: (row_blk0 + b, u_col // ct + c)),
            pl.BlockSpec((t_len, ct), lambda b, c: (row_blk0 + b, g_col // ct + c)),
            pl.BlockSpec((None, SUBLANES, ct), lambda b, c: (b, 0, c)),
            pl.BlockSpec((None, 1, ct), lambda b, c: (b, 0, c)),
            pl.BlockSpec((CONV_W, ct), lambda b, c: (0, c)),
            vec(),
            pl.BlockSpec((nb_blk, LANES, LANES), lambda b, c: (c, 0, 0)),
            vec(),
            pl.BlockSpec((nb_blk, LANES, LANES), lambda b, c: (c, 0, 0)),
            vec(),
            vec(),
        ],
        out_specs=[
            pl.BlockSpec((t_len, ct), lambda b, c: (b, c)),
            pl.BlockSpec((None, SUBLANES, ct), lambda b, c: (b, 0, c)),
            pl.BlockSpec((None, 1, ct), lambda b, c: (b, 0, c)),
        ],
        out_shape=[
            jax.ShapeDtypeStruct((n_b * t_len, lru_w), jnp.bfloat16),
            jax.ShapeDtypeStruct((n_b, SUBLANES, lru_w), jnp.float32),
            jax.ShapeDtypeStruct((n_b, 1, lru_w), jnp.float32),
        ],
        scratch_shapes=[
            pltpu.VMEM((t_len + SUBLANES, ct), jnp.float32),
            pltpu.VMEM((t_len, ct), jnp.float32),
            pltpu.VMEM((t_len, ct), jnp.float32),
            pltpu.VMEM((t_len, ct), jnp.float32),
        ],
        compiler_params=_params(("parallel", "parallel")),
        name="rg_lru",
    )(z, z, cbuf, h0, cw, cb, gaw, gab, gxw, gxb, lam)


def _tree(x, op):
    while x.shape[0] > SUBLANES:
        half = x.shape[0] // 2
        x = op(x[:half], x[half:])
    return x


def _key_to_float(key):
    return pltpu.bitcast(jnp.where(key < 0, key ^ 0x7FFFFFFF, key), jnp.float32)


def _attn_kernel(*refs, tqv, n_past, t_cur, pos0, n_sel, n_heads):
    has_past = n_past > 0
    n_in = 10 if has_past else 7
    qi_ref, q_ref, ga_ref, wt_ref, kcur_ref, vcur_ref, kicur_ref = refs[:7]
    if has_past:
        kp_ref, vp_ref, kip_ref = refs[7:10]
    y_ref = refs[n_in]
    kb_ref, vt_ref, kib_ref, qis_ref, qs_ref, sc_ref, m_ref, s_ref, ot_ref, stage_ref = refs[n_in + 1:]
    tq = LANES
    kb = KEY_BLOCK
    group = n_heads // N_KV_HEADS
    s_all = n_past + t_cur
    s_pad = sc_ref.shape[0]
    b = pl.program_id(0)
    qb = pl.program_id(1)

    def load_block(k_src, v_src, ki_src, src_row, dst_row):
        rows = slice(src_row, src_row + LANES)
        kib_ref[dst_row:dst_row + LANES, :] = ki_src[rows, 0:IDX_DIM].astype(jnp.bfloat16)
        for g in range(N_KV_HEADS):
            cols = slice(g * HEAD_DIM, (g + 1) * HEAD_DIM)
            kb_ref[g, dst_row:dst_row + LANES, :] = k_src[rows, cols].astype(jnp.bfloat16)
            lane0 = dst_row % kb
            vt_ref[g, dst_row // kb, :, lane0:lane0 + LANES] = v_src[rows, cols].T.astype(jnp.bfloat16)

    @pl.when(qb == 0)
    def _assemble_keys():
        if has_past:
            for blk in range(n_past // LANES):
                load_block(kp_ref, vp_ref, kip_ref, blk * LANES, blk * LANES)
        if t_cur % kb == 0:
            for blk in range(t_cur // LANES):
                load_block(kcur_ref, vcur_ref, kicur_ref, blk * LANES, n_past + blk * LANES)
        else:
            kv_w = N_KV_HEADS * HEAD_DIM
            stage_ref[...] = jnp.zeros(stage_ref.shape, jnp.float32)
            stage_ref[0, 0:t_cur, 0:kv_w] = kcur_ref[...]
            stage_ref[1, 0:t_cur, 0:kv_w] = vcur_ref[...]
            stage_ref[2, 0:t_cur, 0:IDX_DIM] = kicur_ref[...]
            for blk in range((s_pad - n_past) // LANES):
                load_block(stage_ref.at[0], stage_ref.at[1], stage_ref.at[2], blk * LANES,
                           n_past + blk * LANES)

    if tqv < tq:
        @pl.when((b == 0) & (qb == 0))
        def _zero_query_padding():
            qis_ref[...] = jnp.zeros(qis_ref.shape, qis_ref.dtype)
            qs_ref[...] = jnp.zeros(qs_ref.shape, qs_ref.dtype)

    for h in range(N_IDX_HEADS):
        qis_ref[h * tq:h * tq + tqv, :] = qi_ref[:, h * IDX_DIM:(h + 1) * IDX_DIM].astype(jnp.bfloat16)
    q_scale = LOG2_E * HEAD_DIM ** -0.5
    for h in range(n_heads):
        qs_ref[h * tq:h * tq + tqv, :] = (
            q_ref[:, h * HEAD_DIM:(h + 1) * HEAD_DIM] * q_scale).astype(jnp.bfloat16)

    w = wt_ref[...] * (N_IDX_HEADS ** -0.5)
    qpos = pos0 + qb * tqv + lax.broadcasted_iota(jnp.int32, (1, tq), 1)
    qchunk = lax.shift_right_logical(qpos, CHUNK_SHIFT)

    last_q = pos0 + qb * tqv + (tqv - 1)
    n_allowed = jnp.minimum(
        lax.shift_left(lax.shift_right_logical(last_q, CHUNK_SHIFT) + 1, CHUNK_SHIFT), s_all)
    n_chunks = lax.shift_right_logical(n_allowed + (kb - 1), KEY_BLOCK_SHIFT)

    def chunk_rows(c):
        return pl.ds(pl.multiple_of(c * kb, kb), kb)

    def score_chunk(c, carry):
        rows = chunk_rows(c)
        kic = kib_ref[rows, :]
        sc = jnp.zeros((kb, tq), jnp.float32)
        for hg in range(N_IDX_HEADS // 4):
            rel = lax.dot_general(kic, qis_ref[hg * 4 * tq:(hg + 1) * 4 * tq, :], _NT,
                                  preferred_element_type=jnp.float32)
            for j in range(4):
                h = hg * 4 + j
                sc = sc + jnp.maximum(rel[:, j * tq:(j + 1) * tq], 0.0) * w[h:h + 1, :]
        sc = sc * (IDX_DIM ** -0.5)
        kpos = c * kb + lax.broadcasted_iota(jnp.int32, (kb, tq), 0)
        allowed = (lax.shift_right_logical(kpos, CHUNK_SHIFT) <= qchunk) & (kpos < s_all)
        sc_ref[rows, :] = jnp.where(allowed, sc, -jnp.inf)
        return carry

    lax.fori_loop(0, n_chunks, score_chunk, 0)

    def count_ge(cand_f):
        def body(c, acc):
            return acc + _tree(jnp.where(sc_ref[chunk_rows(c), :] >= cand_f, 1.0, 0.0), jnp.add)
        acc = lax.fori_loop(0, n_chunks, body, jnp.zeros((SUBLANES, tq), jnp.float32))
        return jnp.sum(acc, axis=0, keepdims=True)

    thr = jnp.where(count_ge(jnp.zeros((1, tq), jnp.float32)) >= n_sel, 0, INT_MIN).astype(jnp.int32)

    def bit_body(i, t):
        cand = t + lax.shift_left(jnp.int32(1), 30 - i)
        return jnp.where(count_ge(_key_to_float(cand)) >= n_sel, cand, t)

    thr = lax.fori_loop(0, 31, bit_body, thr)
    thr_f = _key_to_float(jnp.maximum(thr, KEY_NEG_INF + 1))

    m_ref[...] = jnp.full(m_ref.shape, -jnp.inf, jnp.float32)
    s_ref[...] = jnp.zeros(s_ref.shape, jnp.float32)
    ot_ref[...] = jnp.zeros(ot_ref.shape, jnp.float32)
    slopes2 = [LOG2_E * 2.0 ** (-8.0 * (h + 1) / n_heads) for h in range(n_heads)]

    def attn_chunk(c, carry):
        rows = chunk_rows(c)
        kpos = c * kb + lax.broadcasted_iota(jnp.int32, (kb, tq), 0)
        dist = jnp.abs(qpos - kpos).astype(jnp.float32)
        dmc = jnp.where(sc_ref[rows, :] >= thr_f, dist, jnp.inf)
        for g in range(N_KV_HEADS):
            qg = qs_ref[g * group * tq:(g + 1) * group * tq, :]
            lg = lax.dot_general(kb_ref[g, rows, :], qg, _NT, preferred_element_type=jnp.float32)
            ps, alphas = [], []
            for j in range(group):
                h = g * group + j
                lj = lg[:, j * tq:(j + 1) * tq] - slopes2[h] * dmc
                m_old = m_ref[h:h + 1, :]
                m_new = jnp.maximum(m_old, jnp.max(_tree(lj, jnp.maximum), axis=0, keepdims=True))
                m_safe = jnp.where(m_new == -jnp.inf, 0.0, m_new)
                alpha = jnp.exp2(m_old - m_safe)
                p = jnp.exp2(lj - m_safe)
                srows = slice(h * SUBLANES, (h + 1) * SUBLANES)
                s_ref[srows, :] = s_ref[srows, :] * alpha + _tree(p, jnp.add)
                m_ref[h:h + 1, :] = m_new
                ps.append(p.astype(jnp.bfloat16))
                alphas.append(alpha)
            pv = jnp.dot(vt_ref[g, c], jnp.concatenate(ps, axis=1), preferred_element_type=jnp.float32)
            ot_ref[g] = ot_ref[g] * jnp.concatenate(alphas, axis=1) + pv
        return carry

    lax.fori_loop(0, n_chunks, attn_chunk, 0)

    for h in range(n_heads):
        g, j = divmod(h, group)
        s = jnp.sum(s_ref[h * SUBLANES:(h + 1) * SUBLANES, :], axis=0, keepdims=True)
        o = (ot_ref[g, :, j * tq:(j + 1) * tq] / s).T
        gate = ga_ref[:, h * HEAD_DIM:(h + 1) * HEAD_DIM]
        y_ref[:, h * HEAD_DIM:(h + 1) * HEAD_DIM] = (
            o[0:tqv, :] * (gate * jax.nn.sigmoid(gate))).astype(y_ref.dtype)


def _attn_call(z, ki, wt, past, *, layer, row_blk0, n_b, n_q, tqv, t_cur, cur_blk0, pos0, cols):
    qi_col, q_col, ga_col, k_col, v_col = cols
    n_heads = (ga_col - q_col) // HEAD_DIM
    att_w = n_heads * HEAD_DIM
    qi_w = N_IDX_HEADS * IDX_DIM
    kv_w = N_KV_HEADS * HEAD_DIM
    n_past = 0 if past is None else past[0].shape[2]
    s_all = n_past + t_cur
    cur_pad = ((t_cur + KEY_BLOCK - 1) // KEY_BLOCK) * KEY_BLOCK
    s_pad = n_past + cur_pad
    n_sel = min(TOPK_MAX, s_all // 4)
    stage_shape = (3, cur_pad, kv_w) if t_cur % KEY_BLOCK else (3, SUBLANES, LANES)
    kern = functools.partial(_attn_kernel, tqv=tqv, n_past=n_past, t_cur=t_cur, pos0=pos0,
                             n_sel=n_sel, n_heads=n_heads)
    qrow = lambda b, q: row_blk0 + b * n_q + q
    in_specs = [
        pl.BlockSpec((tqv, qi_w), lambda b, q: (qrow(b, q), qi_col // qi_w)),
        pl.BlockSpec((tqv, att_w), lambda b, q: (qrow(b, q), q_col // att_w)),
        pl.BlockSpec((tqv, att_w), lambda b, q: (qrow(b, q), ga_col // att_w)),
        pl.BlockSpec((N_IDX_HEADS, LANES), lambda b, q: (0, b * n_q + q)),
        pl.BlockSpec((t_cur, kv_w), lambda b, q: (cur_blk0 + b, k_col // kv_w)),
        pl.BlockSpec((t_cur, kv_w), lambda b, q: (cur_blk0 + b, v_col // kv_w)),
        pl.BlockSpec((t_cur, IDX_DIM), lambda b, q: (cur_blk0 + b, 0)),
    ]
    args = [z, z, z, wt, z, z, ki]
    if past is not None:
        in_specs += [
            pl.BlockSpec((None, None, n_past, kv_w), lambda b, q: (layer, b, 0, 0)),
            pl.BlockSpec((None, None, n_past, kv_w), lambda b, q: (layer, b, 0, 0)),
            pl.BlockSpec((None, None, n_past, IDX_DIM), lambda b, q: (layer, b, 0, 0)),
        ]
        args += list(past)
    return pl.pallas_call(
        kern,
        grid=(n_b, n_q),
        in_specs=in_specs,
        out_specs=pl.BlockSpec((tqv, att_w), lambda b, q: (b * n_q + q, 0)),
        out_shape=jax.ShapeDtypeStruct((n_b * n_q * tqv, att_w), jnp.bfloat16),
        scratch_shapes=[
            pltpu.VMEM((N_KV_HEADS, s_pad, HEAD_DIM), jnp.bfloat16),
            pltpu.VMEM((N_KV_HEADS, s_pad // KEY_BLOCK, HEAD_DIM, KEY_BLOCK), jnp.bfloat16),
            pltpu.VMEM((s_pad, IDX_DIM), jnp.bfloat16),
            pltpu.VMEM((N_IDX_HEADS * LANES, IDX_DIM), jnp.bfloat16),
            pltpu.VMEM((n_heads * LANES, HEAD_DIM), jnp.bfloat16),
            pltpu.VMEM((s_pad, LANES), jnp.float32),
            pltpu.VMEM((n_heads, LANES), jnp.float32),
            pltpu.VMEM((n_heads * SUBLANES, LANES), jnp.float32),
            pltpu.VMEM((N_KV_HEADS, HEAD_DIM, (n_heads // N_KV_HEADS) * LANES), jnp.float32),
            pltpu.VMEM(stage_shape, jnp.float32),
        ],
        compiler_params=_params(("arbitrary", "arbitrary")),
        name="sparse_attn",
    )(*args)


def _out_kernel(yl_ref, ya_ref, xb_ref, p_ref, xf_ref, wo_ref, wg_ref, wp_ref, o_ref, *, alpha):
    half = yl_ref.shape[1]
    mix = (jnp.dot(yl_ref[...], wo_ref[0:half, :], preferred_element_type=jnp.float32)
           + jnp.dot(ya_ref[...], wo_ref[half:, :], preferred_element_type=jnp.float32))
    gate = jax.nn.sigmoid(jnp.dot(xb_ref[...], wg_ref[...], preferred_element_type=jnp.float32))
    emb = jnp.dot(p_ref[...], wp_ref[...], preferred_element_type=jnp.float32)
    o_ref[...] = alpha * xf_ref[...] + mix + gate * emb


def _out_call(yl, ya, xb, p, xf, wo, wg, wp, alpha, tm, tn):
    m, d = xf.shape
    half = yl.shape[1]
    ple = p.shape[1]
    row = lambda w: pl.BlockSpec((tm, w), lambda i, j: (i, 0))
    col = lambda k: pl.BlockSpec((k, tn), lambda i, j: (0, j))
    return pl.pallas_call(
        functools.partial(_out_kernel, alpha=alpha),
        grid=(m // tm, d // tn),
        in_specs=[row(half), row(ya.shape[1]), row(d), row(ple),
                  pl.BlockSpec((tm, tn), lambda i, j: (i, j)),
                  col(wo.shape[0]), col(d), col(ple)],
        out_specs=pl.BlockSpec((tm, tn), lambda i, j: (i, j)),
        out_shape=jax.ShapeDtypeStruct((m, d), jnp.float32),
        compiler_params=_params(("parallel", "parallel")),
        name="out_proj",
    )(yl, ya, xb, p, xf, wo, wg, wp)


def _ln_kernel(x_ref, g_ref, b_ref, of_ref, ob_ref, *, rs):
    def sub_block(c, carry):
        rows = pl.ds(pl.multiple_of(c * rs, rs), rs)
        x = x_ref[rows, :]
        mu = jnp.mean(x, axis=-1, keepdims=True)
        xc = x - mu
        var = jnp.mean(xc * xc, axis=-1, keepdims=True)
        y = xc * lax.rsqrt(var + LN_EPS) * g_ref[...] + b_ref[...]
        of_ref[rows, :] = y
        ob_ref[rows, :] = y.astype(ob_ref.dtype)
        return carry

    lax.fori_loop(0, x_ref.shape[0] // rs, sub_block, 0)


def _ln_call(x, g, b, tr):
    m, d = x.shape
    return pl.pallas_call(
        functools.partial(_ln_kernel, rs=_pick_tile(tr, (32, 16))),
        grid=(m // tr,),
        in_specs=[pl.BlockSpec((tr, d), lambda i: (i, 0)),
                  pl.BlockSpec((1, d), lambda i: (0, 0)),
                  pl.BlockSpec((1, d), lambda i: (0, 0))],
        out_specs=[pl.BlockSpec((tr, d), lambda i: (i, 0)),
                   pl.BlockSpec((tr, d), lambda i: (i, 0))],
        out_shape=[jax.ShapeDtypeStruct((m, d), jnp.float32),
                   jax.ShapeDtypeStruct((m, d), jnp.bfloat16)],
        compiler_params=_params(("parallel",)),
        name="layer_norm",
    )(x, g, b)


def kernel(x_prompt, x_sample, cache_k, cache_v, cache_kidx, state_conv, state_h, p_prompt, p_sample, w_in, conv_w, conv_b, gate_a_w, gate_a_b, gate_x_w, gate_x_b, lru_lambda, w_out, ln_g, ln_b, ple_proj, ple_gate):
    bf16 = jnp.bfloat16
    n_bp, t_p, d = x_prompt.shape
    n_bs, t_s, _ = x_sample.shape
    depth = w_in.shape[0]
    n_past = cache_k.shape[2]
    lru_w = conv_w.shape[2]
    att_w = d - lru_w
    n_heads = att_w // HEAD_DIM
    kv_w = N_KV_HEADS * HEAD_DIM
    qi_w = N_IDX_HEADS * IDX_DIM
    alpha = (2 * depth) ** 0.25
    m_p, m_s = n_bp * t_p, n_bs * t_s
    m = m_p + m_s
    assert t_p % KEY_BLOCK == 0 and m_p % t_s == 0 and t_s % SUBLANES == 0 and n_past % KEY_BLOCK == 0
    assert lru_w // N_LRU_BLOCKS == LANES and CHUNK == 1 << CHUNK_SHIFT

    o_u, o_gl = 0, lru_w
    o_q = 2 * lru_w
    o_k = o_q + att_w
    o_v = o_k + kv_w
    o_ga = o_v + kv_w
    o_qi = o_ga + att_w
    o_ki = o_qi + qi_w
    c_qi, c_q = 0, qi_w
    c_ga = c_q + att_w
    c_u = c_ga + att_w
    c_gl = c_u + lru_w
    c_k = c_gl + lru_w
    c_v = c_k + kv_w
    n_main = c_v + kv_w

    tm = _pick_tile(m, (768, 512, 384, 256, 128))
    tn_in = _pick_tile(n_main, (1024, 512))
    tn_out = _pick_tile(d, (512,))
    tr_ln = _pick_tile(m, (256, 128))
    ct_p = _pick_tile(lru_w, (256,))
    tq = LANES
    n_q = t_p // tq

    xf = jnp.concatenate([x_prompt.reshape(m_p, d), x_sample.reshape(m_s, d)], axis=0)
    xb = xf.astype(bf16)
    hist_pad = ((0, 0), (SUBLANES - (CONV_W - 1), 0), (0, 0))
    zero_conv = jnp.zeros((n_bp, SUBLANES, lru_w), jnp.float32)
    zero_h = jnp.zeros((n_bp, 1, lru_w), jnp.float32)
    past = (cache_k.reshape(depth, n_bs, n_past, kv_w), cache_v.reshape(depth, n_bs, n_past, kv_w), cache_kidx)

    outs_p, outs_s = [], []
    for i in range(depth):
        wl = w_in[i]
        seg = lambda o, n: wl[:, o:o + n]
        w_main = jnp.concatenate(
            [seg(o_qi, qi_w), seg(o_q, att_w), seg(o_ga, att_w), seg(o_u, lru_w), seg(o_gl, lru_w),
             seg(o_k, kv_w), seg(o_v, kv_w)], axis=1).astype(bf16)
        w_side = wl[:, o_ki:].astype(bf16)

        z = _matmul(xb, w_main, tm, tn_in)
        zs = _matmul(xb, w_side, tm, w_side.shape[1])
        ki = zs[:, :IDX_DIM]
        wi = zs[:, IDX_DIM:]
        wt_p = wi[:m_p].T
        wt_s = jnp.pad(wi[m_p:].reshape(n_bs, t_s, N_IDX_HEADS),
                       ((0, 0), (0, LANES - t_s), (0, 0))).reshape(n_bs * LANES, N_IDX_HEADS).T

        lru_args = (conv_w[i], conv_b[i][None], gate_a_w[i].astype(bf16), gate_a_b[i][None],
                    gate_x_w[i].astype(bf16), gate_x_b[i][None], lru_lambda[i][None])
        yl_p, conv_p, h_p = _lru_call(z, 0, n_bp, t_p, ct_p, c_u, c_gl, zero_conv, zero_h, *lru_args, pos0=0)
        yl_s, conv_s, h_s = _lru_call(z, m_p // t_s, n_bs, t_s, lru_w, c_u, c_gl,
                                      jnp.pad(state_conv[i], hist_pad),
                                      state_h[i][:, None, :], *lru_args, pos0=n_past)
        conv_p, conv_s = conv_p[:, :CONV_W - 1], conv_s[:, :CONV_W - 1]

        cols = (c_qi, c_q, c_ga, c_k, c_v)
        ya_p = _attn_call(z, ki, wt_p, None, layer=i, row_blk0=0, n_b=n_bp, n_q=n_q, tqv=tq, t_cur=t_p,
                          cur_blk0=0, pos0=0, cols=cols)
        ya_s = _attn_call(z, ki, wt_s, past, layer=i, row_blk0=m_p // t_s, n_b=n_bs, n_q=1, tqv=t_s,
                          t_cur=t_s, cur_blk0=m_p // t_s, pos0=n_past, cols=cols)

        yl = jnp.concatenate([yl_p, yl_s], axis=0)
        ya = jnp.concatenate([ya_p, ya_s], axis=0)
        p = jnp.concatenate([p_prompt[i].reshape(m_p, -1), p_sample[i].reshape(m_s, -1)], axis=0).astype(bf16)
        pre = _out_call(yl, ya, xb, p, xf, w_out[i].astype(bf16), ple_gate[i].astype(bf16),
                        ple_proj[i].astype(bf16), alpha, tm, tn_out)
        xf, xb = _ln_call(pre, ln_g[i][None], ln_b[i][None], tr_ln)

        outs_p.append((z[:m_p, c_k:c_k + kv_w].reshape(n_bp, t_p, N_KV_HEADS, HEAD_DIM),
                       z[:m_p, c_v:c_v + kv_w].reshape(n_bp, t_p, N_KV_HEADS, HEAD_DIM),
                       ki[:m_p].reshape(n_bp, t_p, IDX_DIM), conv_p, h_p[:, 0, :]))
        outs_s.append((z[m_p:, c_k:c_k + kv_w].reshape(n_bs, t_s, N_KV_HEADS, HEAD_DIM),
                       z[m_p:, c_v:c_v + kv_w].reshape(n_bs, t_s, N_KV_HEADS, HEAD_DIM),
                       ki[m_p:].reshape(n_bs, t_s, IDX_DIM), conv_s, h_s[:, 0, :]))

    stack = lambda outs, j: jnp.stack([o[j] for o in outs])
    y_p = xf[:m_p].reshape(n_bp, t_p, d)
    y_s = xf[m_p:].reshape(n_bs, t_s, d)
    return (y_p, y_s,
            stack(outs_p, 0), stack(outs_p, 1), stack(outs_p, 2), stack(outs_p, 3), stack(outs_p, 4),
            stack(outs_s, 0), stack(outs_s, 1), stack(outs_s, 2), stack(outs_s, 3), stack(outs_s, 4))
```

```python
import functools

import jax
import jax.numpy as jnp
from jax import lax
from jax.experimental import pallas as pl
from jax.experimental.pallas import tpu as pltpu

CHUNK = 64
CHUNK_SHIFT = 6
N_LRU_BLOCKS = 16
CONV_W = 4
LRU_C = 8.0
HEAD_DIM = 128
N_KV_HEADS = 4
N_IDX_HEADS = 32
IDX_DIM = 128
TOPK_MAX = 256
LN_EPS = 1e-5
LOG2_E = 1.4426950408889634

LANES = 128
SUBLANES = 8
VMEM_LIMIT_BYTES = 56 * 1024 * 1024

KEY_BLOCK = 256
KEY_BLOCK_SHIFT = 8

INT_MIN = -(2 ** 31)
KEY_NEG_INF = -2139095041

_NT = (((1,), (1,)), ((), ()))


def _pick_tile(n, candidates):
    for c in candidates:
        if n % c == 0:
            return c
    return n


def _params(sem):
    return pltpu.CompilerParams(dimension_semantics=sem, vmem_limit_bytes=VMEM_LIMIT_BYTES)


def _mm_kernel(x_ref, w_ref, o_ref):
    o_ref[...] = jnp.dot(x_ref[...], w_ref[...], preferred_element_type=jnp.float32)


def _matmul(x, w, tm, tn):
    m, k = x.shape
    n = w.shape[1]
    return pl.pallas_call(
        _mm_kernel,
        grid=(m // tm, n // tn),
        in_specs=[pl.BlockSpec((tm, k), lambda i, j: (i, 0)),
                  pl.BlockSpec((k, tn), lambda i, j: (0, j))],
        out_specs=pl.BlockSpec((tm, tn), lambda i, j: (i, j)),
        out_shape=jax.ShapeDtypeStruct((m, n), jnp.float32),
        compiler_params=_params(("parallel", "parallel")),
        name="in_proj",
    )(x, w)


def _sigmoid(x):
    return 0.5 * jnp.tanh(0.5 * x) + 0.5


def _lru_kernel(u_ref, g_ref, cbuf_ref, h0_ref, cw_ref, cb_ref, gaw_ref, gab_ref,
                gxw_ref, gxb_ref, lam_ref, y_ref, cnew_ref, hlast_ref,
                full_ref, *, t_len, ct, rc, pos0):
    full_ref[0:SUBLANES, :] = cbuf_ref[...]
    full_ref[SUBLANES:SUBLANES + t_len, :] = u_ref[...]
    cnew_ref[...] = pltpu.roll(full_ref[t_len:t_len + SUBLANES, :], CONV_W - 1, 0)

    n_tiles = rc // SUBLANES
    row = lax.broadcasted_iota(jnp.int32, (SUBLANES, ct), 0)
    cw = cw_ref[...]
    cb = cb_ref[...]
    nlam = -lam_ref[...]
    softplus = jnp.maximum(nlam, 0.0) + jnp.log1p(jnp.exp(-jnp.abs(nlam)))
    gab = gab_ref[...]
    gxb = gxb_ref[...]

    def chunk(c, h):
        r0 = pl.multiple_of(c * rc, rc)
        conv_tiles = []
        for t in range(n_tiles):
            prev = full_ref[pl.ds(r0 + t * SUBLANES, SUBLANES), :]
            cur = full_ref[pl.ds(r0 + (t + 1) * SUBLANES, SUBLANES), :]
            acc = cb + cur * cw[CONV_W - 1:CONV_W, :]
            for s in range(1, CONV_W):
                shifted = pltpu.roll(jnp.where(row >= SUBLANES - s, prev, cur), s, 0)
                acc = acc + shifted * cw[CONV_W - 1 - s:CONV_W - s, :]
            conv_tiles.append(acc)
        conv = jnp.concatenate(conv_tiles, axis=0) if n_tiles > 1 else conv_tiles[0]
        conv16 = conv.astype(jnp.bfloat16)
        r_parts, i_parts = [], []
        for n in range(ct // LANES):
            ub = conv16[:, n * LANES:(n + 1) * LANES]
            r_parts.append(jnp.dot(ub, gaw_ref[n], preferred_element_type=jnp.float32))
            i_parts.append(jnp.dot(ub, gxw_ref[n], preferred_element_type=jnp.float32))
        r = _sigmoid(jnp.concatenate(r_parts, axis=1) + gab)
        gi = _sigmoid(jnp.concatenate(i_parts, axis=1) + gxb)
        log_a = -LRU_C * r * softplus
        th = jnp.tanh(log_a)
        mult = jnp.sqrt(-2.0 * th / (1.0 - th))
        a = jnp.exp(log_a)
        b = conv * gi * mult

        scanned = []
        for t in range(n_tiles):
            ac = a[t * SUBLANES:(t + 1) * SUBLANES, :]
            bc = b[t * SUBLANES:(t + 1) * SUBLANES, :]
            if t == 0 and pos0 == 0:
                start = (row == 0) & (r0 == 0)
                bc = jnp.where(start, conv[0:SUBLANES, :] * gi[0:SUBLANES, :], bc)
            for s in (1, 2, 4):
                a_sh = jnp.where(row >= s, pltpu.roll(ac, s, 0), 1.0)
                b_sh = jnp.where(row >= s, pltpu.roll(bc, s, 0), 0.0)
                bc = ac * b_sh + bc
                ac = ac * a_sh
            scanned.append((ac, bc))
        h_rows = []
        for ac, bc in scanned:
            h_rows.append(ac * h + bc)
            h = ac[SUBLANES - 1:SUBLANES, :] * h + bc[SUBLANES - 1:SUBLANES, :]
        h_seq = jnp.concatenate(h_rows, axis=0) if n_tiles > 1 else h_rows[0]
        g = g_ref[pl.ds(r0, rc), :]
        y_ref[pl.ds(r0, rc), :] = (h_seq * (g * _sigmoid(g))).astype(y_ref.dtype)
        return h

    hlast_ref[...] = lax.fori_loop(0, t_len // rc, chunk, h0_ref[...])


def _drop_ref(fn, pos):
    def body(*refs):
        return fn(*refs[:pos], *refs[pos + 1:])
    return body


def _lru_call(z, row_blk0, n_b, t_len, ct, u_col, g_col, cbuf, h0, cw, cb, gaw, gab, gxw, gxb, lam, pos0,
              y_rows, y_prev=None):
    lru_w = cw.shape[1]
    n_c = lru_w // ct
    nb_blk = ct // LANES
    rc = _pick_tile(t_len, (128, 64, 32, 16))
    kern = functools.partial(_lru_kernel, t_len=t_len, ct=ct, rc=rc, pos0=pos0)
    vec = lambda: pl.BlockSpec((1, ct), lambda b, c: (0, c))
    args = [z, z, cbuf, h0, cw, cb, gaw, gab, gxw, gxb, lam]
    extra_specs, aliases = [], {}
    if y_prev is not None:
        kern = _drop_ref(kern, len(args))
        extra_specs = [pl.BlockSpec(memory_space=pl.ANY)]
        aliases = {len(args): 0}
        args.append(y_prev)
    return pl.pallas_call(
        kern,
        grid=(n_b, n_c),
        input_output_aliases=aliases,
        in_specs=[
            pl.BlockSpec((t_len, ct), lambda b, c: (row_blk0 + b, u_col // ct + c)),
            pl.BlockSpec((t_len, ct), lambda b, c: (row_blk0 + b, g_col // ct + c)),
            pl.BlockSpec((None, SUBLANES, ct), lambda b, c: (b, 0, c)),
            pl.BlockSpec((None, 1, ct), lambda b, c: (b, 0, c)),
            pl.BlockSpec((CONV_W, ct), lambda b, c: (0, c)),
            vec(),
            pl.BlockSpec((nb_blk, LANES, LANES), lambda b, c: (c, 0, 0)),
            vec(),
            pl.BlockSpec((nb_blk, LANES, LANES), lambda b, c: (c, 0, 0)),
            vec(),
            vec(),
        ] + extra_specs,
        out_specs=[
            pl.BlockSpec((t_len, ct), lambda b, c: (row_blk0 + b, c)),
            pl.BlockSpec((None, SUBLANES, ct), lambda b, c: (b, 0, c)),
            pl.BlockSpec((None, 1, ct), lambda b, c: (b, 0, c)),
        ],
        out_shape=[
            jax.ShapeDtypeStruct((y_rows, lru_w), jnp.bfloat16),
            jax.ShapeDtypeStruct((n_b, SUBLANES, lru_w), jnp.float32),
            jax.ShapeDtypeStruct((n_b, 1, lru_w), jnp.float32),
        ],
        scratch_shapes=[
            pltpu.VMEM((t_len + SUBLANES, ct), jnp.float32),
        ],
        compiler_params=_params(("parallel", "parallel")),
        name="rg_lru",
    )(*args)


def _tree(x, op):
    while x.shape[0] > SUBLANES:
        half = x.shape[0] // 2
        x = op(x[:half], x[half:])
    return x


def _key_to_float(key):
    return pltpu.bitcast(jnp.where(key < 0, key ^ 0x7FFFFFFF, key), jnp.float32)


def _attn_kernel(*refs, tqv, n_past, t_cur, pos0, n_sel, n_heads):
    has_past = n_past > 0
    n_in = 10 if has_past else 7
    qi_ref, q_ref, ga_ref, wt_ref, kcur_ref, vcur_ref, kicur_ref = refs[:7]
    if has_past:
        kp_ref, vp_ref, kip_ref = refs[7:10]
    y_ref = refs[n_in]
    kb_ref, vt_ref, kib_ref, qis_ref, qs_ref, sc_ref, m_ref, s_ref, ot_ref, stage_ref = refs[n_in + 1:]
    tq = LANES
    kb = KEY_BLOCK
    group = n_heads // N_KV_HEADS
    s_all = n_past + t_cur
    s_pad = sc_ref.shape[0]
    b = pl.program_id(0)
    qb = pl.program_id(1)

    def load_block(k_src, v_src, ki_src, src_row, dst_row, heads_on_rows=False):
        rows = slice(src_row, src_row + LANES)
        kib_ref[dst_row:dst_row + LANES, :] = ki_src[rows, 0:IDX_DIM].astype(jnp.bfloat16)
        for g in range(N_KV_HEADS):
            if heads_on_rows:
                idx = (pl.ds(src_row * N_KV_HEADS + g, LANES, stride=N_KV_HEADS), slice(None))
            else:
                idx = (rows, slice(g * HEAD_DIM, (g + 1) * HEAD_DIM))
            kb_ref[g, dst_row:dst_row + LANES, :] = k_src[idx].astype(jnp.bfloat16)
            lane0 = dst_row % kb
            vt_ref[g, dst_row // kb, :, lane0:lane0 + LANES] = v_src[idx].T.astype(jnp.bfloat16)

    @pl.when(qb == 0)
    def _assemble_keys():
        if has_past:
            for blk in range(n_past // LANES):
                load_block(kp_ref, vp_ref, kip_ref, blk * LANES, blk * LANES, heads_on_rows=True)
        if t_cur % kb == 0:
            for blk in range(t_cur // LANES):
                load_block(kcur_ref, vcur_ref, kicur_ref, blk * LANES, n_past + blk * LANES)
        else:
            kv_w = N_KV_HEADS * HEAD_DIM
            stage_ref[...] = jnp.zeros(stage_ref.shape, jnp.float32)
            stage_ref[0, 0:t_cur, 0:kv_w] = kcur_ref[...]
            stage_ref[1, 0:t_cur, 0:kv_w] = vcur_ref[...]
            stage_ref[2, 0:t_cur, 0:IDX_DIM] = kicur_ref[...]
            for blk in range((s_pad - n_past) // LANES):
                load_block(stage_ref.at[0], stage_ref.at[1], stage_ref.at[2], blk * LANES,
                           n_past + blk * LANES)

    if tqv < tq:
        @pl.when((b == 0) & (qb == 0))
        def _zero_query_padding():
            qis_ref[...] = jnp.zeros(qis_ref.shape, qis_ref.dtype)
            qs_ref[...] = jnp.zeros(qs_ref.shape, qs_ref.dtype)

    for h in range(N_IDX_HEADS):
        qis_ref[h * tq:h * tq + tqv, :] = qi_ref[:, h * IDX_DIM:(h + 1) * IDX_DIM].astype(jnp.bfloat16)
    q_scale = LOG2_E * HEAD_DIM ** -0.5
    for h in range(n_heads):
        qs_ref[h * tq:h * tq + tqv, :] = (
            q_ref[:, h * HEAD_DIM:(h + 1) * HEAD_DIM] * q_scale).astype(jnp.bfloat16)

    w = wt_ref[...] * (N_IDX_HEADS ** -0.5)
    qpos = pos0 + qb * tqv + lax.broadcasted_iota(jnp.int32, (1, tq), 1)
    qchunk = lax.shift_right_logical(qpos, CHUNK_SHIFT)

    last_q = pos0 + qb * tqv + (tqv - 1)
    n_allowed = jnp.minimum(
        lax.shift_left(lax.shift_right_logical(last_q, CHUNK_SHIFT) + 1, CHUNK_SHIFT), s_all)
    n_chunks = lax.shift_right_logical(n_allowed + (kb - 1), KEY_BLOCK_SHIFT)

    def chunk_rows(c):
        return pl.ds(pl.multiple_of(c * kb, kb), kb)

    def score_chunk(c, carry):
        rows = chunk_rows(c)
        kic = kib_ref[rows, :]
        sc = jnp.zeros((kb, tq), jnp.float32)
        for hg in range(N_IDX_HEADS // 4):
            rel = lax.dot_general(kic, qis_ref[hg * 4 * tq:(hg + 1) * 4 * tq, :], _NT,
                                  preferred_element_type=jnp.float32)
            for j in range(4):
                h = hg * 4 + j
                sc = sc + jnp.maximum(rel[:, j * tq:(j + 1) * tq], 0.0) * w[h:h + 1, :]
        sc = sc * (IDX_DIM ** -0.5)
        kpos = c * kb + lax.broadcasted_iota(jnp.int32, (kb, tq), 0)
        allowed = (lax.shift_right_logical(kpos, CHUNK_SHIFT) <= qchunk) & (kpos < s_all)
        sc_ref[rows, :] = jnp.where(allowed, sc, -jnp.inf)
        return carry

    lax.fori_loop(0, n_chunks, score_chunk, 0)

    def count_ge(cand_f):
        def body(c, acc):
            return acc + _tree(jnp.where(sc_ref[chunk_rows(c), :] >= cand_f, 1.0, 0.0), jnp.add)
        acc = lax.fori_loop(0, n_chunks, body, jnp.zeros((SUBLANES, tq), jnp.float32))
        return jnp.sum(acc, axis=0, keepdims=True)

    thr = jnp.where(count_ge(jnp.zeros((1, tq), jnp.float32)) >= n_sel, 0, INT_MIN).astype(jnp.int32)

    def bit_body(i, t):
        cand = t + lax.shift_left(jnp.int32(1), 30 - i)
        return jnp.where(count_ge(_key_to_float(cand)) >= n_sel, cand, t)

    thr = lax.fori_loop(0, 31, bit_body, thr)
    thr_f = _key_to_float(jnp.maximum(thr, KEY_NEG_INF + 1))

    m_ref[...] = jnp.full(m_ref.shape, -jnp.inf, jnp.float32)
    s_ref[...] = jnp.zeros(s_ref.shape, jnp.float32)
    ot_ref[...] = jnp.zeros(ot_ref.shape, jnp.float32)
    slopes2 = [LOG2_E * 2.0 ** (-8.0 * (h + 1) / n_heads) for h in range(n_heads)]

    def attn_chunk(c, carry):
        rows = chunk_rows(c)
        kpos = c * kb + lax.broadcasted_iota(jnp.int32, (kb, tq), 0)
        dist = jnp.abs(qpos - kpos).astype(jnp.float32)
        dmc = jnp.where(sc_ref[rows, :] >= thr_f, dist, jnp.inf)
        for g in range(N_KV_HEADS):
            qg = qs_ref[g * group * tq:(g + 1) * group * tq, :]
            lg = lax.dot_general(kb_ref[g, rows, :], qg, _NT, preferred_element_type=jnp.float32)
            ps, alphas = [], []
            for j in range(group):
                h = g * group + j
                lj = lg[:, j * tq:(j + 1) * tq] - slopes2[h] * dmc
                m_old = m_ref[h:h + 1, :]
                m_new = jnp.maximum(m_old, jnp.max(_tree(lj, jnp.maximum), axis=0, keepdims=True))
                m_safe = jnp.where(m_new == -jnp.inf, 0.0, m_new)
                alpha = jnp.exp2(m_old - m_safe)
                p = jnp.exp2(lj - m_safe)
                srows = slice(h * SUBLANES, (h + 1) * SUBLANES)
                s_ref[srows, :] = s_ref[srows, :] * alpha + _tree(p, jnp.add)
                m_ref[h:h + 1, :] = m_new
                ps.append(p.astype(jnp.bfloat16))
                alphas.append(alpha)
            pv = jnp.dot(vt_ref[g, c], jnp.concatenate(ps, axis=1), preferred_element_type=jnp.float32)
            ot_ref[g] = ot_ref[g] * jnp.concatenate(alphas, axis=1) + pv
        return carry

    lax.fori_loop(0, n_chunks, attn_chunk, 0)

    for h in range(n_heads):
        g, j = divmod(h, group)
        s = jnp.sum(s_ref[h * SUBLANES:(h + 1) * SUBLANES, :], axis=0, keepdims=True)
        o = (ot_ref[g, :, j * tq:(j + 1) * tq] / s).T
        gate = ga_ref[:, h * HEAD_DIM:(h + 1) * HEAD_DIM]
        y_ref[:, h * HEAD_DIM:(h + 1) * HEAD_DIM] = (
            o[0:tqv, :] * (gate * jax.nn.sigmoid(gate))).astype(y_ref.dtype)


def _attn_call(z, ki, wt, past, *, layer, row_blk0, n_b, n_q, tqv, t_cur, cur_blk0, pos0, cols,
               y_rows, y_prev=None):
    qi_col, q_col, ga_col, k_col, v_col = cols
    n_heads = (ga_col - q_col) // HEAD_DIM
    att_w = n_heads * HEAD_DIM
    qi_w = N_IDX_HEADS * IDX_DIM
    kv_w = N_KV_HEADS * HEAD_DIM
    n_past = 0 if past is None else past[2].shape[2]
    s_all = n_past + t_cur
    cur_pad = ((t_cur + KEY_BLOCK - 1) // KEY_BLOCK) * KEY_BLOCK
    s_pad = n_past + cur_pad
    n_sel = min(TOPK_MAX, s_all // 4)
    stage_shape = (3, cur_pad, kv_w) if t_cur % KEY_BLOCK else (3, SUBLANES, LANES)
    kern = functools.partial(_attn_kernel, tqv=tqv, n_past=n_past, t_cur=t_cur, pos0=pos0,
                             n_sel=n_sel, n_heads=n_heads)
    qrow = lambda b, q: row_blk0 + b * n_q + q
    in_specs = [
        pl.BlockSpec((tqv, qi_w), lambda b, q: (qrow(b, q), qi_col // qi_w)),
        pl.BlockSpec((tqv, att_w), lambda b, q: (qrow(b, q), q_col // att_w)),
        pl.BlockSpec((tqv, att_w), lambda b, q: (qrow(b, q), ga_col // att_w)),
        pl.BlockSpec((N_IDX_HEADS, LANES), lambda b, q: (0, b * n_q + q)),
        pl.BlockSpec((t_cur, kv_w), lambda b, q: (cur_blk0 + b, k_col // kv_w)),
        pl.BlockSpec((t_cur, kv_w), lambda b, q: (cur_blk0 + b, v_col // kv_w)),
        pl.BlockSpec((t_cur, IDX_DIM), lambda b, q: (cur_blk0 + b, 0)),
    ]
    args = [z, z, z, wt, z, z, ki]
    if past is not None:
        in_specs += [
            pl.BlockSpec((None, None, n_past * N_KV_HEADS, HEAD_DIM), lambda b, q: (layer, b, 0, 0)),
            pl.BlockSpec((None, None, n_past * N_KV_HEADS, HEAD_DIM), lambda b, q: (layer, b, 0, 0)),
            pl.BlockSpec((None, None, n_past, IDX_DIM), lambda b, q: (layer, b, 0, 0)),
        ]
        args += list(past)
    aliases = {}
    if y_prev is not None:
        kern = _drop_ref(kern, len(args))
        in_specs.append(pl.BlockSpec(memory_space=pl.ANY))
        aliases = {len(args): 0}
        args.append(y_prev)
    return pl.pallas_call(
        kern,
        grid=(n_b, n_q),
        input_output_aliases=aliases,
        in_specs=in_specs,
        out_specs=pl.BlockSpec((tqv, att_w), lambda b, q: (qrow(b, q), 0)),
        out_shape=jax.ShapeDtypeStruct((y_rows, att_w), jnp.bfloat16),
        scratch_shapes=[
            pltpu.VMEM((N_KV_HEADS, s_pad, HEAD_DIM), jnp.bfloat16),
            pltpu.VMEM((N_KV_HEADS, s_pad // KEY_BLOCK, HEAD_DIM, KEY_BLOCK), jnp.bfloat16),
            pltpu.VMEM((s_pad, IDX_DIM), jnp.bfloat16),
            pltpu.VMEM((N_IDX_HEADS * LANES, IDX_DIM), jnp.bfloat16),
            pltpu.VMEM((n_heads * LANES, HEAD_DIM), jnp.bfloat16),
            pltpu.VMEM((s_pad, LANES), jnp.float32),
            pltpu.VMEM((n_heads, LANES), jnp.float32),
            pltpu.VMEM((n_heads * SUBLANES, LANES), jnp.float32),
            pltpu.VMEM((N_KV_HEADS, HEAD_DIM, (n_heads // N_KV_HEADS) * LANES), jnp.float32),
            pltpu.VMEM(stage_shape, jnp.float32),
        ],
        compiler_params=_params(("arbitrary", "arbitrary")),
        name="sparse_attn",
    )(*args)


def _out_kernel(yl_ref, ya_ref, xb_ref, p_ref, xf_ref, wo_ref, wg_ref, wp_ref, o_ref, *, alpha):
    half = yl_ref.shape[1]
    mix = (jnp.dot(yl_ref[...], wo_ref[0:half, :], preferred_element_type=jnp.float32)
           + jnp.dot(ya_ref[...], wo_ref[half:, :], preferred_element_type=jnp.float32))
    gate = jax.nn.sigmoid(jnp.dot(xb_ref[...], wg_ref[...], preferred_element_type=jnp.float32))
    emb = jnp.dot(p_ref[...], wp_ref[...], preferred_element_type=jnp.float32)
    o_ref[...] = alpha * xf_ref[...] + mix + gate * emb


def _out_call(yl, ya, xb, p, xf, wo, wg, wp, alpha, tm, tn):
    m, d = xf.shape
    half = yl.shape[1]
    ple = p.shape[1]
    row = lambda w: pl.BlockSpec((tm, w), lambda i, j: (i, 0))
    col = lambda k: pl.BlockSpec((k, tn), lambda i, j: (0, j))
    return pl.pallas_call(
        functools.partial(_out_kernel, alpha=alpha),
        grid=(m // tm, d // tn),
        in_specs=[row(half), row(ya.shape[1]), row(d), row(ple),
                  pl.BlockSpec((tm, tn), lambda i, j: (i, j)),
                  col(wo.shape[0]), col(d), col(ple)],
        out_specs=pl.BlockSpec((tm, tn), lambda i, j: (i, j)),
        out_shape=jax.ShapeDtypeStruct((m, d), jnp.float32),
        compiler_params=_params(("parallel", "parallel")),
        name="out_proj",
    )(yl, ya, xb, p, xf, wo, wg, wp)


def _ln_kernel(x_ref, g_ref, b_ref, of_ref, ob_ref, *, rs):
    def sub_block(c, carry):
        rows = pl.ds(pl.multiple_of(c * rs, rs), rs)
        x = x_ref[rows, :]
        mu = jnp.mean(x, axis=-1, keepdims=True)
        xc = x - mu
        var = jnp.mean(xc * xc, axis=-1, keepdims=True)
        y = xc * lax.rsqrt(var + LN_EPS) * g_ref[...] + b_ref[...]
        of_ref[rows, :] = y
        ob_ref[rows, :] = y.astype(ob_ref.dtype)
        return carry

    lax.fori_loop(0, x_ref.shape[0] // rs, sub_block, 0)


def _ln_call(x, g, b, tr):
    m, d = x.shape
    return pl.pallas_call(
        functools.partial(_ln_kernel, rs=_pick_tile(tr, (32, 16))),
        grid=(m // tr,),
        in_specs=[pl.BlockSpec((tr, d), lambda i: (i, 0)),
                  pl.BlockSpec((1, d), lambda i: (0, 0)),
                  pl.BlockSpec((1, d), lambda i: (0, 0))],
        out_specs=[pl.BlockSpec((tr, d), lambda i: (i, 0)),
                   pl.BlockSpec((tr, d), lambda i: (i, 0))],
        out_shape=[jax.ShapeDtypeStruct((m, d), jnp.float32),
                   jax.ShapeDtypeStruct((m, d), jnp.bfloat16)],
        compiler_params=_params(("parallel",)),
        name="layer_norm",
    )(x, g, b)


def kernel(x_prompt, x_sample, cache_k, cache_v, cache_kidx, state_conv, state_h, p_prompt, p_sample, w_in, conv_w, conv_b, gate_a_w, gate_a_b, gate_x_w, gate_x_b, lru_lambda, w_out, ln_g, ln_b, ple_proj, ple_gate):
    bf16 = jnp.bfloat16
    n_bp, t_p, d = x_prompt.shape
    n_bs, t_s, _ = x_sample.shape
    depth = w_in.shape[0]
    n_past = cache_k.shape[2]
    lru_w = conv_w.shape[2]
    att_w = d - lru_w
    n_heads = att_w // HEAD_DIM
    kv_w = N_KV_HEADS * HEAD_DIM
    qi_w = N_IDX_HEADS * IDX_DIM
    alpha = (2 * depth) ** 0.25
    m_p, m_s = n_bp * t_p, n_bs * t_s
    m = m_p + m_s
    assert t_p % KEY_BLOCK == 0 and m_p % t_s == 0 and t_s % SUBLANES == 0 and n_past % KEY_BLOCK == 0
    assert lru_w // N_LRU_BLOCKS == LANES and CHUNK == 1 << CHUNK_SHIFT

    o_u, o_gl = 0, lru_w
    o_q = 2 * lru_w
    o_k = o_q + att_w
    o_v = o_k + kv_w
    o_ga = o_v + kv_w
    o_qi = o_ga + att_w
    o_ki = o_qi + qi_w
    c_qi, c_q = 0, qi_w
    c_ga = c_q + att_w
    c_u = c_ga + att_w
    c_gl = c_u + lru_w
    c_k = c_gl + lru_w
    c_v = c_k + kv_w
    n_main = c_v + kv_w

    tm = _pick_tile(m, (768, 512, 384, 256, 128))
    tn_in = _pick_tile(n_main, (1024, 512))
    tn_out = _pick_tile(d, (512,))
    tr_ln = _pick_tile(m, (256, 128))
    ct_p = _pick_tile(lru_w, (256,))
    tq = LANES
    n_q = t_p // tq

    xf = jnp.concatenate([x_prompt.reshape(m_p, d), x_sample.reshape(m_s, d)], axis=0)
    xb = xf.astype(bf16)
    hist_pad = ((0, 0), (SUBLANES - (CONV_W - 1), 0), (0, 0))
    zero_conv = jnp.zeros((n_bp, SUBLANES, lru_w), jnp.float32)
    zero_h = jnp.zeros((n_bp, 1, lru_w), jnp.float32)
    past = (cache_k.reshape(depth, n_bs, n_past * N_KV_HEADS, HEAD_DIM),
            cache_v.reshape(depth, n_bs, n_past * N_KV_HEADS, HEAD_DIM), cache_kidx)

    outs_p, outs_s = [], []
    for i in range(depth):
        wl = w_in[i]
        seg = lambda o, n: wl[:, o:o + n]
        w_main = jnp.concatenate(
            [seg(o_qi, qi_w), seg(o_q, att_w), seg(o_ga, att_w), seg(o_u, lru_w), seg(o_gl, lru_w),
             seg(o_k, kv_w), seg(o_v, kv_w)], axis=1).astype(bf16)
        w_side = wl[:, o_ki:].astype(bf16)

        z = _matmul(xb, w_main, tm, tn_in)
        zs = _matmul(xb, w_side, tm, w_side.shape[1])
        ki = zs[:, :IDX_DIM]
        wi = zs[:, IDX_DIM:]
        wt_p = wi[:m_p].T
        wt_s = jnp.pad(wi[m_p:].reshape(n_bs, t_s, N_IDX_HEADS),
                       ((0, 0), (0, LANES - t_s), (0, 0))).reshape(n_bs * LANES, N_IDX_HEADS).T

        lru_args = (conv_w[i], conv_b[i][None], gate_a_w[i].astype(bf16), gate_a_b[i][None],
                    gate_x_w[i].astype(bf16), gate_x_b[i][None], lru_lambda[i][None])
        yl, conv_p, h_p = _lru_call(z, 0, n_bp, t_p, ct_p, c_u, c_gl, zero_conv, zero_h, *lru_args,
                                    pos0=0, y_rows=m)
        yl, conv_s, h_s = _lru_call(z, m_p // t_s, n_bs, t_s, lru_w, c_u, c_gl,
                                    jnp.pad(state_conv[i], hist_pad), state_h[i][:, None, :], *lru_args,
                                    pos0=n_past, y_rows=m, y_prev=yl)
        conv_p, conv_s = conv_p[:, :CONV_W - 1], conv_s[:, :CONV_W - 1]

        cols = (c_qi, c_q, c_ga, c_k, c_v)
        ya = _attn_call(z, ki, wt_p, None, layer=i, row_blk0=0, n_b=n_bp, n_q=n_q, tqv=tq, t_cur=t_p,
                        cur_blk0=0, pos0=0, cols=cols, y_rows=m)
        ya = _attn_call(z, ki, wt_s, past, layer=i, row_blk0=m_p // t_s, n_b=n_bs, n_q=1, tqv=t_s,
                        t_cur=t_s, cur_blk0=m_p // t_s, pos0=n_past, cols=cols, y_rows=m, y_prev=ya)

        p = jnp.concatenate([p_prompt[i].reshape(m_p, -1), p_sample[i].reshape(m_s, -1)], axis=0).astype(bf16)
        pre = _out_call(yl, ya, xb, p, xf, w_out[i].astype(bf16), ple_gate[i].astype(bf16),
                        ple_proj[i].astype(bf16), alpha, tm, tn_out)
        xf, xb = _ln_call(pre, ln_g[i][None], ln_b[i][None], tr_ln)

        outs_p.append((z[:m_p, c_k:c_k + kv_w].reshape(n_bp, t_p, N_KV_HEADS, HEAD_DIM),
                       z[:m_p, c_v:c_v + kv_w].reshape(n_bp, t_p, N_KV_HEADS, HEAD_DIM),
                       ki[:m_p].reshape(n_bp, t_p, IDX_DIM), conv_p, h_p[:, 0, :]))
        outs_s.append((z[m_p:, c_k:c_k + kv_w].reshape(n_bs, t_s, N_KV_HEADS, HEAD_DIM),
                       z[m_p:, c_v:c_v + kv_w].reshape(n_bs, t_s, N_KV_HEADS, HEAD_DIM),
                       ki[m_p:].reshape(n_bs, t_s, IDX_DIM), conv_s, h_s[:, 0, :]))

    stack = lambda outs, j: jnp.stack([o[j] for o in outs])
    y_p = xf[:m_p].reshape(n_bp, t_p, d)
    y_s = xf[m_p:].reshape(n_bs, t_s, d)
    return (y_p, y_s,
            stack(outs_p, 0), stack(outs_p, 1), stack(outs_p, 2), stack(outs_p, 3), stack(outs_p, 4),
            stack(outs_s, 0), stack(outs_s, 1), stack(outs_s, 2), stack(outs_s, 3), stack(outs_s, 4))
```

```python
import functools

import jax
import jax.numpy as jnp
from jax import lax
from jax.experimental import pallas as pl
from jax.experimental.pallas import tpu as pltpu

CHUNK = 64
CHUNK_SHIFT = 6
N_LRU_BLOCKS = 16
CONV_W = 4
LRU_C = 8.0
HEAD_DIM = 128
N_KV_HEADS = 4
N_IDX_HEADS = 32
IDX_DIM = 128
TOPK_MAX = 256
LN_EPS = 1e-5
LOG2_E = 1.4426950408889634

LANES = 128
SUBLANES = 8
VMEM_LIMIT_BYTES = 56 * 1024 * 1024

KEY_BLOCK = 256
KEY_BLOCK_SHIFT = 8

INT_MIN = -(2 ** 31)
KEY_NEG_INF = -2139095041

_NT = (((1,), (1,)), ((), ()))


def _pick_tile(n, candidates):
    for c in candidates:
        if n % c == 0:
            return c
    return n


def _params(sem):
    return pltpu.CompilerParams(dimension_semantics=sem, vmem_limit_bytes=VMEM_LIMIT_BYTES)


def _mm_kernel(x_ref, w_ref, o_ref):
    o_ref[...] = jnp.dot(x_ref[...], w_ref[...], preferred_element_type=jnp.float32)


def _matmul(x, w, tm, tn):
    m, k = x.shape
    n = w.shape[1]
    return pl.pallas_call(
        _mm_kernel,
        grid=(m // tm, n // tn),
        in_specs=[pl.BlockSpec((tm, k), lambda i, j: (i, 0)),
                  pl.BlockSpec((k, tn), lambda i, j: (0, j))],
        out_specs=pl.BlockSpec((tm, tn), lambda i, j: (i, j)),
        out_shape=jax.ShapeDtypeStruct((m, n), jnp.float32),
        compiler_params=_params(("parallel", "parallel")),
        name="in_proj",
    )(x, w)


def _sigmoid(x):
    return 0.5 * jnp.tanh(0.5 * x) + 0.5


def _lru_kernel(u_ref, g_ref, cbuf_ref, h0_ref, cw_ref, cb_ref, gaw_ref, gab_ref,
                gxw_ref, gxb_ref, lam_ref, y_ref, cnew_ref, hlast_ref,
                full_ref, *, t_len, ct, rc, pos0):
    full_ref[0:SUBLANES, :] = cbuf_ref[...]
    full_ref[SUBLANES:SUBLANES + t_len, :] = u_ref[...]
    cnew_ref[...] = pltpu.roll(full_ref[t_len:t_len + SUBLANES, :], CONV_W - 1, 0)

    n_tiles = rc // SUBLANES
    row = lax.broadcasted_iota(jnp.int32, (SUBLANES, ct), 0)
    cw = cw_ref[...]
    cb = cb_ref[...]
    nlam = -lam_ref[...]
    softplus = jnp.maximum(nlam, 0.0) + jnp.log1p(jnp.exp(-jnp.abs(nlam)))
    gab = gab_ref[...]
    gxb = gxb_ref[...]

    def chunk(c, h):
        r0 = pl.multiple_of(c * rc, rc)
        conv_tiles = []
        for t in range(n_tiles):
            prev = full_ref[pl.ds(r0 + t * SUBLANES, SUBLANES), :]
            cur = full_ref[pl.ds(r0 + (t + 1) * SUBLANES, SUBLANES), :]
            acc = cb + cur * cw[CONV_W - 1:CONV_W, :]
            for s in range(1, CONV_W):
                shifted = pltpu.roll(jnp.where(row >= SUBLANES - s, prev, cur), s, 0)
                acc = acc + shifted * cw[CONV_W - 1 - s:CONV_W - s, :]
            conv_tiles.append(acc)
        conv = jnp.concatenate(conv_tiles, axis=0) if n_tiles > 1 else conv_tiles[0]
        conv16 = conv.astype(jnp.bfloat16)
        r_parts, i_parts = [], []
        for n in range(ct // LANES):
            ub = conv16[:, n * LANES:(n + 1) * LANES]
            r_parts.append(jnp.dot(ub, gaw_ref[n], preferred_element_type=jnp.float32))
            i_parts.append(jnp.dot(ub, gxw_ref[n], preferred_element_type=jnp.float32))
        r = _sigmoid(jnp.concatenate(r_parts, axis=1) + gab)
        gi = _sigmoid(jnp.concatenate(i_parts, axis=1) + gxb)
        log_a = -LRU_C * r * softplus
        th = jnp.tanh(log_a)
        mult = jnp.sqrt(-2.0 * th / (1.0 - th))
        a = jnp.exp(log_a)
        b = conv * gi * mult

        scanned = []
        for t in range(n_tiles):
            ac = a[t * SUBLANES:(t + 1) * SUBLANES, :]
            bc = b[t * SUBLANES:(t + 1) * SUBLANES, :]
            if t == 0 and pos0 == 0:
                start = (row == 0) & (r0 == 0)
                bc = jnp.where(start, conv[0:SUBLANES, :] * gi[0:SUBLANES, :], bc)
            for s in (1, 2, 4):
                a_sh = jnp.where(row >= s, pltpu.roll(ac, s, 0), 1.0)
                b_sh = jnp.where(row >= s, pltpu.roll(bc, s, 0), 0.0)
                bc = ac * b_sh + bc
                ac = ac * a_sh
            scanned.append((ac, bc))
        h_rows = []
        for ac, bc in scanned:
            h_rows.append(ac * h + bc)
            h = ac[SUBLANES - 1:SUBLANES, :] * h + bc[SUBLANES - 1:SUBLANES, :]
        h_seq = jnp.concatenate(h_rows, axis=0) if n_tiles > 1 else h_rows[0]
        g = g_ref[pl.ds(r0, rc), :]
        y_ref[pl.ds(r0, rc), :] = (h_seq * (g * _sigmoid(g))).astype(y_ref.dtype)
        return h

    hlast_ref[...] = lax.fori_loop(0, t_len // rc, chunk, h0_ref[...])


def _drop_ref(fn, pos):
    def body(*refs):
        return fn(*refs[:pos], *refs[pos + 1:])
    return body


def _lru_call(z, row_blk0, n_b, t_len, ct, u_col, g_col, cbuf, h0, cw, cb, gaw, gab, gxw, gxb, lam, pos0,
              y_rows, y_prev=None):
    lru_w = cw.shape[1]
    n_c = lru_w // ct
    nb_blk = ct // LANES
    rc = _pick_tile(t_len, (128, 64, 32, 16))
    kern = functools.partial(_lru_kernel, t_len=t_len, ct=ct, rc=rc, pos0=pos0)
    vec = lambda: pl.BlockSpec((1, ct), lambda b, c: (0, c))
    args = [z, z, cbuf, h0, cw, cb, gaw, gab, gxw, gxb, lam]
    extra_specs, aliases = [], {}
    if y_prev is not None:
        kern = _drop_ref(kern, len(args))
        extra_specs = [pl.BlockSpec(memory_space=pl.ANY)]
        aliases = {len(args): 0}
        args.append(y_prev)
    return pl.pallas_call(
        kern,
        grid=(n_b, n_c),
        input_output_aliases=aliases,
        in_specs=[
            pl.BlockSpec((t_len, ct), lambda b, c: (row_blk0 + b, u_col // ct + c)),
            pl.BlockSpec((t_len, ct), lambda b, c: (row_blk0 + b, g_col // ct + c)),
            pl.BlockSpec((None, SUBLANES, ct), lambda b, c: (b, 0, c)),
            pl.BlockSpec((None, 1, ct), lambda b, c: (b, 0, c)),
            pl.BlockSpec((CONV_W, ct), lambda b, c: (0, c)),
            vec(),
            pl.BlockSpec((nb_blk, LANES, LANES), lambda b, c: (c, 0, 0)),
            vec(),
            pl.BlockSpec((nb_blk, LANES, LANES), lambda b, c: (c, 0, 0)),
            vec(),
            vec(),
        ] + extra_specs,
        out_specs=[
            pl.BlockSpec((t_len, ct), lambda b, c: (row_blk0 + b, c)),
            pl.BlockSpec((None, SUBLANES, ct), lambda b, c: (b, 0, c)),
            pl.BlockSpec((None, 1, ct), lambda b, c: (b, 0, c)),
        ],
        out_shape=[
            jax.ShapeDtypeStruct((y_rows, lru_w), jnp.bfloat16),
            jax.ShapeDtypeStruct((n_b, SUBLANES, lru_w), jnp.float32),
            jax.ShapeDtypeStruct((n_b, 1, lru_w), jnp.float32),
        ],
        scratch_shapes=[
            pltpu.VMEM((t_len + SUBLANES, ct), jnp.float32),
        ],
        compiler_params=_params(("parallel", "parallel")),
        name="rg_lru",
    )(*args)


def _tree(x, op):
    tiles = [x[i:i + SUBLANES] for i in range(0, x.shape[0], SUBLANES)]
    while len(tiles) > 1:
        pairs = [op(tiles[i], tiles[i + 1]) for i in range(0, len(tiles) - 1, 2)]
        tiles = pairs + tiles[len(pairs) * 2:]
    return tiles[0]


def _key_to_float(key):
    return pltpu.bitcast(jnp.where(key < 0, key ^ 0x7FFFFFFF, key), jnp.float32)


def _kth_largest(count_ge, k, lanes):
    thr = jnp.where(count_ge(jnp.zeros((1, lanes), jnp.float32)) >= k, 0, INT_MIN).astype(jnp.int32)

    def bit_body(i, t):
        cand = t + lax.shift_left(jnp.int32(1), 30 - i)
        return jnp.where(count_ge(_key_to_float(cand)) >= k, cand, t)

    thr = lax.fori_loop(0, 31, bit_body, thr)
    return _key_to_float(jnp.maximum(thr, KEY_NEG_INF + 1))


def _n_key_chunks(last_q, s_all):
    n_allowed = jnp.minimum(
        lax.shift_left(lax.shift_right_logical(last_q, CHUNK_SHIFT) + 1, CHUNK_SHIFT), s_all)
    return lax.shift_right_logical(n_allowed + (KEY_BLOCK - 1), KEY_BLOCK_SHIFT)


def _attn_kernel(qi_ref, q_ref, ga_ref, wt_ref, k_ref, v_ref, ki_ref, y_ref,
                 kb_ref, vt_ref, kib_ref, qis_ref, qs_ref, sc_ref, m_ref, s_ref, ot_ref,
                 *, tq, n_sel, n_heads):
    kb = KEY_BLOCK
    group = n_heads // N_KV_HEADS
    s_all = sc_ref.shape[0]
    qb = pl.program_id(1)

    @pl.when(qb == 0)
    def _assemble_keys():
        for blk in range(s_all // LANES):
            rows = slice(blk * LANES, (blk + 1) * LANES)
            kib_ref[rows, :] = ki_ref[rows, :].astype(jnp.bfloat16)
            for g in range(N_KV_HEADS):
                cols = slice(g * HEAD_DIM, (g + 1) * HEAD_DIM)
                kb_ref[g, rows, :] = k_ref[rows, cols].astype(jnp.bfloat16)
                lane0 = (blk * LANES) % kb
                vt_ref[g, (blk * LANES) // kb, :, lane0:lane0 + LANES] = (
                    v_ref[rows, cols].T.astype(jnp.bfloat16))

    for h in range(N_IDX_HEADS):
        qis_ref[h * tq:(h + 1) * tq, :] = qi_ref[:, h * IDX_DIM:(h + 1) * IDX_DIM].astype(jnp.bfloat16)
    q_scale = LOG2_E * HEAD_DIM ** -0.5
    for h in range(n_heads):
        qs_ref[h * tq:(h + 1) * tq, :] = (
            q_ref[:, h * HEAD_DIM:(h + 1) * HEAD_DIM] * q_scale).astype(jnp.bfloat16)

    w = wt_ref[...] * (N_IDX_HEADS ** -0.5)
    qpos = qb * tq + lax.broadcasted_iota(jnp.int32, (1, tq), 1)
    qchunk = lax.shift_right_logical(qpos, CHUNK_SHIFT)
    n_chunks = _n_key_chunks(qb * tq + (tq - 1), s_all)

    def chunk_rows(c):
        return pl.ds(pl.multiple_of(c * kb, kb), kb)

    def score_chunk(c, carry):
        rows = chunk_rows(c)
        kic = kib_ref[rows, :]
        sc = jnp.zeros((kb, tq), jnp.float32)
        for hg in range(N_IDX_HEADS // 4):
            rel = lax.dot_general(kic, qis_ref[hg * 4 * tq:(hg + 1) * 4 * tq, :], _NT,
                                  preferred_element_type=jnp.float32)
            for j in range(4):
                h = hg * 4 + j
                sc = sc + jnp.maximum(rel[:, j * tq:(j + 1) * tq], 0.0) * w[h:h + 1, :]
        sc = sc * (IDX_DIM ** -0.5)
        kpos = c * kb + lax.broadcasted_iota(jnp.int32, (kb, tq), 0)
        allowed = lax.shift_right_logical(kpos, CHUNK_SHIFT) <= qchunk
        sc_ref[rows, :] = jnp.where(allowed, sc, -jnp.inf)
        return carry

    lax.fori_loop(0, n_chunks, score_chunk, 0)

    def count_ge(cand_f):
        def body(c, acc):
            return acc + _tree(jnp.where(sc_ref[chunk_rows(c), :] >= cand_f, 1.0, 0.0), jnp.add)
        acc = lax.fori_loop(0, n_chunks, body, jnp.zeros((SUBLANES, tq), jnp.float32))
        return jnp.sum(acc, axis=0, keepdims=True)

    thr_f = _kth_largest(count_ge, n_sel, tq)

    m_ref[...] = jnp.full(m_ref.shape, -jnp.inf, jnp.float32)
    s_ref[...] = jnp.zeros(s_ref.shape, jnp.float32)
    ot_ref[...] = jnp.zeros(ot_ref.shape, jnp.float32)
    slopes2 = [LOG2_E * 2.0 ** (-8.0 * (h + 1) / n_heads) for h in range(n_heads)]

    def attn_chunk(c, carry):
        rows = chunk_rows(c)
        kpos = c * kb + lax.broadcasted_iota(jnp.int32, (kb, tq), 0)
        dist = jnp.abs(qpos - kpos).astype(jnp.float32)
        dmc = jnp.where(sc_ref[rows, :] >= thr_f, dist, jnp.inf)
        for g in range(N_KV_HEADS):
            qg = qs_ref[g * group * tq:(g + 1) * group * tq, :]
            lg = lax.dot_general(kb_ref[g, rows, :], qg, _NT, preferred_element_type=jnp.float32)
            ps, alphas = [], []
            for j in range(group):
                h = g * group + j
                lj = lg[:, j * tq:(j + 1) * tq] - slopes2[h] * dmc
                m_old = m_ref[h:h + 1, :]
                m_new = jnp.maximum(m_old, jnp.max(_tree(lj, jnp.maximum), axis=0, keepdims=True))
                m_safe = jnp.where(m_new == -jnp.inf, 0.0, m_new)
                alpha = jnp.exp2(m_old - m_safe)
                p = jnp.exp2(lj - m_safe)
                srows = slice(h * SUBLANES, (h + 1) * SUBLANES)
                s_ref[srows, :] = s_ref[srows, :] * alpha + _tree(p, jnp.add)
                m_ref[h:h + 1, :] = m_new
                ps.append(p.astype(jnp.bfloat16))
                alphas.append(alpha)
            pv = jnp.dot(vt_ref[g, c], jnp.concatenate(ps, axis=1), preferred_element_type=jnp.float32)
            ot_ref[g] = ot_ref[g] * jnp.concatenate(alphas, axis=1) + pv
        return carry

    lax.fori_loop(0, n_chunks, attn_chunk, 0)

    for h in range(n_heads):
        g, j = divmod(h, group)
        s = jnp.sum(s_ref[h * SUBLANES:(h + 1) * SUBLANES, :], axis=0, keepdims=True)
        o = (ot_ref[g, :, j * tq:(j + 1) * tq] / s).T
        gate = ga_ref[:, h * HEAD_DIM:(h + 1) * HEAD_DIM]
        y_ref[:, h * HEAD_DIM:(h + 1) * HEAD_DIM] = (o * (gate * _sigmoid(gate))).astype(y_ref.dtype)


def _attn_call(z, ki, wt, *, n_b, t_len, tq, cols, y_rows):
    qi_col, q_col, ga_col, k_col, v_col = cols
    n_heads = (ga_col - q_col) // HEAD_DIM
    att_w = n_heads * HEAD_DIM
    qi_w = N_IDX_HEADS * IDX_DIM
    kv_w = N_KV_HEADS * HEAD_DIM
    n_q = t_len // tq
    n_sel = min(TOPK_MAX, t_len // 4)
    kern = functools.partial(_attn_kernel, tq=tq, n_sel=n_sel, n_heads=n_heads)
    qrow = lambda b, q: b * n_q + q
    return pl.pallas_call(
        kern,
        grid=(n_b, n_q),
        in_specs=[
            pl.BlockSpec((tq, qi_w), lambda b, q: (qrow(b, q), qi_col // qi_w)),
            pl.BlockSpec((tq, att_w), lambda b, q: (qrow(b, q), q_col // att_w)),
            pl.BlockSpec((tq, att_w), lambda b, q: (qrow(b, q), ga_col // att_w)),
            pl.BlockSpec((N_IDX_HEADS, tq), lambda b, q: (0, qrow(b, q))),
            pl.BlockSpec((t_len, kv_w), lambda b, q: (b, k_col // kv_w)),
            pl.BlockSpec((t_len, kv_w), lambda b, q: (b, v_col // kv_w)),
            pl.BlockSpec((t_len, IDX_DIM), lambda b, q: (b, 0)),
        ],
        out_specs=pl.BlockSpec((tq, att_w), lambda b, q: (qrow(b, q), 0)),
        out_shape=jax.ShapeDtypeStruct((y_rows, att_w), jnp.bfloat16),
        scratch_shapes=[
            pltpu.VMEM((N_KV_HEADS, t_len, HEAD_DIM), jnp.bfloat16),
            pltpu.VMEM((N_KV_HEADS, t_len // KEY_BLOCK, HEAD_DIM, KEY_BLOCK), jnp.bfloat16),
            pltpu.VMEM((t_len, IDX_DIM), jnp.bfloat16),
            pltpu.VMEM((N_IDX_HEADS * tq, IDX_DIM), jnp.bfloat16),
            pltpu.VMEM((n_heads * tq, HEAD_DIM), jnp.bfloat16),
            pltpu.VMEM((t_len, tq), jnp.float32),
            pltpu.VMEM((n_heads, tq), jnp.float32),
            pltpu.VMEM((n_heads * SUBLANES, tq), jnp.float32),
            pltpu.VMEM((N_KV_HEADS, HEAD_DIM, (n_heads // N_KV_HEADS) * tq), jnp.float32),
        ],
        compiler_params=_params(("arbitrary", "arbitrary")),
        name="sparse_attn",
    )(z, z, z, wt, z, z, ki)


def _decode_attn_kernel(qi_ref, q_ref, ga_ref, wrow_ref, kcur_ref, vcur_ref, kicur_ref,
                        kp_ref, vp_ref, kip_ref, y_ref,
                        kb_ref, vt_ref, kib_ref, qit_ref, bdt_ref, sc_ref, stage_ref,
                        *, t_q, n_past, pos0, n_sel, n_heads):
    kb = KEY_BLOCK
    group = n_heads // N_KV_HEADS
    hpt = LANES // t_q
    n_tiles = n_heads // hpt
    gpt = hpt // group
    s_all = n_past + t_q
    s_pad = sc_ref.shape[0]
    kv_w = N_KV_HEADS * HEAD_DIM
    tile_w = gpt * HEAD_DIM
    slot_shifts = [LANES >> (i + 1) for i in range(hpt.bit_length() - 1)]

    def sum_over_slots(x):
        for sh in slot_shifts:
            x = x + pltpu.roll(x, sh, 1)
        return x

    def load_block(k_src, v_src, ki_src, src_row, dst_row, heads_on_rows):
        rows = slice(src_row, src_row + LANES)
        dst = slice(dst_row, dst_row + LANES)
        kib_ref[dst, :] = ki_src[rows, 0:IDX_DIM].astype(jnp.bfloat16)
        for g in range(N_KV_HEADS):
            if heads_on_rows:
                idx = (pl.ds(src_row * N_KV_HEADS + g, LANES, stride=N_KV_HEADS), slice(None))
            else:
                idx = (rows, slice(g * HEAD_DIM, (g + 1) * HEAD_DIM))
            kb_ref[dst, g * HEAD_DIM:(g + 1) * HEAD_DIM] = k_src[idx].astype(jnp.bfloat16)
            lane0 = dst_row % kb
            vt_ref[g // gpt, dst_row // kb, (g % gpt) * HEAD_DIM:(g % gpt + 1) * HEAD_DIM,
                   lane0:lane0 + LANES] = v_src[idx].T.astype(jnp.bfloat16)

    for blk in range(n_past // LANES):
        load_block(kp_ref, vp_ref, kip_ref, blk * LANES, blk * LANES, True)
    stage_ref[...] = jnp.zeros(stage_ref.shape, jnp.float32)
    stage_ref[0, 0:t_q, 0:kv_w] = kcur_ref[...]
    stage_ref[1, 0:t_q, 0:kv_w] = vcur_ref[...]
    stage_ref[2, 0:t_q, 0:IDX_DIM] = kicur_ref[...]
    for blk in range((s_pad - n_past) // LANES):
        load_block(stage_ref.at[0], stage_ref.at[1], stage_ref.at[2], blk * LANES, n_past + blk * LANES,
                   False)

    for h in range(N_IDX_HEADS):
        qit_ref[h * t_q:(h + 1) * t_q, :] = qi_ref[:, h * IDX_DIM:(h + 1) * IDX_DIM].astype(jnp.bfloat16)
    bdt_ref[...] = jnp.zeros(bdt_ref.shape, bdt_ref.dtype)
    q_scale = LOG2_E * HEAD_DIM ** -0.5
    for h in range(n_heads):
        tile, slot = divmod(h, hpt)
        gl = slot // group
        bdt_ref[tile, slot * t_q:(slot + 1) * t_q, gl * HEAD_DIM:(gl + 1) * HEAD_DIM] = (
            q_ref[:, h * HEAD_DIM:(h + 1) * HEAD_DIM] * q_scale).astype(jnp.bfloat16)

    lane = lax.broadcasted_iota(jnp.int32, (1, LANES), 1)
    slot_l = lax.shift_right_logical(lane, t_q.bit_length() - 1)
    qpos = pos0 + (lane & (t_q - 1))
    qchunk = lax.shift_right_logical(qpos, CHUNK_SHIFT)
    w = wrow_ref[...] * (N_IDX_HEADS ** -0.5)
    n_chunks = s_pad // kb

    for c in range(n_chunks):
        rows = slice(c * kb, (c + 1) * kb)
        rel = lax.dot_general(kib_ref[rows, :], qit_ref[...], _NT, preferred_element_type=jnp.float32)
        x = jnp.maximum(rel, 0.0) * w
        acc = x[:, 0:LANES]
        for i in range(1, x.shape[1] // LANES):
            acc = acc + x[:, i * LANES:(i + 1) * LANES]
        sc = sum_over_slots(acc) * (IDX_DIM ** -0.5)
        kpos = c * kb + lax.broadcasted_iota(jnp.int32, (kb, LANES), 0)
        allowed = (lax.shift_right_logical(kpos, CHUNK_SHIFT) <= qchunk) & (kpos < s_all)
        sc_ref[rows, :] = jnp.where(allowed, sc, -jnp.inf)

    def count_ge(cand_f):
        return jnp.sum(_tree(jnp.where(sc_ref[...] >= cand_f, 1.0, 0.0), jnp.add), axis=0, keepdims=True)

    thr_f = _kth_largest(count_ge, n_sel, LANES)

    slopes2 = []
    for p in range(n_tiles):
        head = (p * hpt + slot_l + 1).astype(jnp.float32)
        slopes2.append(LOG2_E * jnp.exp2(head * (-8.0 / n_heads)))
    m = [jnp.full((1, LANES), -jnp.inf, jnp.float32) for _ in range(n_tiles)]
    s8 = [jnp.zeros((SUBLANES, LANES), jnp.float32) for _ in range(n_tiles)]
    ot = [jnp.zeros((tile_w, LANES), jnp.float32) for _ in range(n_tiles)]
    for c in range(n_chunks):
        rows = slice(c * kb, (c + 1) * kb)
        kpos = c * kb + lax.broadcasted_iota(jnp.int32, (kb, LANES), 0)
        dist = jnp.abs(qpos - kpos).astype(jnp.float32)
        dmc = jnp.where(sc_ref[rows, :] >= thr_f, dist, jnp.inf)
        for p in range(n_tiles):
            lg = lax.dot_general(kb_ref[rows, p * tile_w:(p + 1) * tile_w], bdt_ref[p], _NT,
                                 preferred_element_type=jnp.float32)
            lj = lg - slopes2[p] * dmc
            m_new = jnp.maximum(m[p], jnp.max(_tree(lj, jnp.maximum), axis=0, keepdims=True))
            m_safe = jnp.where(m_new == -jnp.inf, 0.0, m_new)
            alpha = jnp.exp2(m[p] - m_safe)
            pr = jnp.exp2(lj - m_safe)
            s8[p] = s8[p] * alpha + _tree(pr, jnp.add)
            m[p] = m_new
            pv = jnp.dot(vt_ref[p, c], pr.astype(jnp.bfloat16), preferred_element_type=jnp.float32)
            ot[p] = ot[p] * alpha + pv

    for p in range(n_tiles):
        o = ot[p] / jnp.sum(s8[p], axis=0, keepdims=True)
        for gl in range(gpt):
            o_t = o[gl * HEAD_DIM:(gl + 1) * HEAD_DIM, :].T
            for j in range(group):
                slot = gl * group + j
                h = p * hpt + slot
                gate = ga_ref[:, h * HEAD_DIM:(h + 1) * HEAD_DIM]
                y_ref[:, h * HEAD_DIM:(h + 1) * HEAD_DIM] = (
                    o_t[slot * t_q:(slot + 1) * t_q, :] * (gate * _sigmoid(gate))).astype(y_ref.dtype)


def _decode_attn_call(z, ki, wrow, past, y_prev, *, layer, row_blk0, n_b, t_q, pos0, cols):
    qi_col, q_col, ga_col, k_col, v_col = cols
    n_heads = (ga_col - q_col) // HEAD_DIM
    att_w = n_heads * HEAD_DIM
    qi_w = N_IDX_HEADS * IDX_DIM
    kv_w = N_KV_HEADS * HEAD_DIM
    group = n_heads // N_KV_HEADS
    hpt = LANES // t_q
    assert t_q & (t_q - 1) == 0 and LANES % t_q == 0 and n_heads % hpt == 0 and hpt % group == 0
    assert (N_IDX_HEADS * t_q) % LANES == 0
    n_past = past[2].shape[2]
    s_all = n_past + t_q
    cur_pad = ((t_q + KEY_BLOCK - 1) // KEY_BLOCK) * KEY_BLOCK
    s_pad = n_past + cur_pad
    n_sel = min(TOPK_MAX, s_all // 4)
    n_tiles = n_heads // hpt
    gpt = hpt // group
    kern = functools.partial(_decode_attn_kernel, t_q=t_q, n_past=n_past, pos0=pos0, n_sel=n_sel,
                             n_heads=n_heads)
    row = lambda b: row_blk0 + b
    n_args = 10
    return pl.pallas_call(
        _drop_ref(kern, n_args),
        grid=(n_b,),
        input_output_aliases={n_args: 0},
        in_specs=[
            pl.BlockSpec((t_q, qi_w), lambda b: (row(b), qi_col // qi_w)),
            pl.BlockSpec((t_q, att_w), lambda b: (row(b), q_col // att_w)),
            pl.BlockSpec((t_q, att_w), lambda b: (row(b), ga_col // att_w)),
            pl.BlockSpec((None, 1, N_IDX_HEADS * t_q), lambda b: (b, 0, 0)),
            pl.BlockSpec((t_q, kv_w), lambda b: (row(b), k_col // kv_w)),
            pl.BlockSpec((t_q, kv_w), lambda b: (row(b), v_col // kv_w)),
            pl.BlockSpec((t_q, IDX_DIM), lambda b: (row(b), 0)),
            pl.BlockSpec((None, None, n_past * N_KV_HEADS, HEAD_DIM), lambda b: (layer, b, 0, 0)),
            pl.BlockSpec((None, None, n_past * N_KV_HEADS, HEAD_DIM), lambda b: (layer, b, 0, 0)),
            pl.BlockSpec((None, None, n_past, IDX_DIM), lambda b: (layer, b, 0, 0)),
            pl.BlockSpec(memory_space=pl.ANY),
        ],
        out_specs=pl.BlockSpec((t_q, att_w), lambda b: (row(b), 0)),
        out_shape=jax.ShapeDtypeStruct(y_prev.shape, y_prev.dtype),
        scratch_shapes=[
            pltpu.VMEM((s_pad, kv_w), jnp.bfloat16),
            pltpu.VMEM((N_KV_HEADS // gpt, s_pad // KEY_BLOCK, gpt * HEAD_DIM, KEY_BLOCK), jnp.bfloat16),
            pltpu.VMEM((s_pad, IDX_DIM), jnp.bfloat16),
            pltpu.VMEM((N_IDX_HEADS * t_q, IDX_DIM), jnp.bfloat16),
            pltpu.VMEM((n_tiles, LANES, gpt * HEAD_DIM), jnp.bfloat16),
            pltpu.VMEM((s_pad, LANES), jnp.float32),
            pltpu.VMEM((3, cur_pad, kv_w), jnp.float32),
        ],
        compiler_params=_params(("parallel",)),
        name="decode_attn",
    )(z, z, z, wrow, z, z, ki, *past, y_prev)


def _out_kernel(yl_ref, ya_ref, xb_ref, p_ref, xf_ref, wo_ref, wg_ref, wp_ref, o_ref, *, alpha):
    half = yl_ref.shape[1]
    mix = (jnp.dot(yl_ref[...], wo_ref[0:half, :], preferred_element_type=jnp.float32)
           + jnp.dot(ya_ref[...], wo_ref[half:, :], preferred_element_type=jnp.float32))
    gate = jax.nn.sigmoid(jnp.dot(xb_ref[...], wg_ref[...], preferred_element_type=jnp.float32))
    emb = jnp.dot(p_ref[...], wp_ref[...], preferred_element_type=jnp.float32)
    o_ref[...] = alpha * xf_ref[...] + mix + gate * emb


def _out_call(yl, ya, xb, p, xf, wo, wg, wp, alpha, tm, tn):
    m, d = xf.shape
    half = yl.shape[1]
    ple = p.shape[1]
    row = lambda w: pl.BlockSpec((tm, w), lambda i, j: (i, 0))
    col = lambda k: pl.BlockSpec((k, tn), lambda i, j: (0, j))
    return pl.pallas_call(
        functools.partial(_out_kernel, alpha=alpha),
        grid=(m // tm, d // tn),
        in_specs=[row(half), row(ya.shape[1]), row(d), row(ple),
                  pl.BlockSpec((tm, tn), lambda i, j: (i, j)),
                  col(wo.shape[0]), col(d), col(ple)],
        out_specs=pl.BlockSpec((tm, tn), lambda i, j: (i, j)),
        out_shape=jax.ShapeDtypeStruct((m, d), jnp.float32),
        compiler_params=_params(("parallel", "parallel")),
        name="out_proj",
    )(yl, ya, xb, p, xf, wo, wg, wp)


def _ln_rows(x_ref, g_ref, b_ref, out_refs, rs):
    def sub_block(c, carry):
        rows = pl.ds(pl.multiple_of(c * rs, rs), rs)
        x = x_ref[rows, :]
        mu = jnp.mean(x, axis=-1, keepdims=True)
        xc = x - mu
        var = jnp.mean(xc * xc, axis=-1, keepdims=True)
        y = xc * lax.rsqrt(var + LN_EPS) * g_ref[...] + b_ref[...]
        for o_ref in out_refs:
            o_ref[rows, :] = y.astype(o_ref.dtype)
        return carry

    lax.fori_loop(0, x_ref.shape[0] // rs, sub_block, 0)


def _ln_kernel(x_ref, g_ref, b_ref, of_ref, ob_ref, *, rs):
    _ln_rows(x_ref, g_ref, b_ref, (of_ref, ob_ref), rs)


def _ln_split_kernel(x_ref, g_ref, b_ref, head_ref, tail_ref, *, rs, n_head):
    i = pl.program_id(0)

    @pl.when(i < n_head)
    def _head():
        _ln_rows(x_ref, g_ref, b_ref, (head_ref,), rs)

    @pl.when(i >= n_head)
    def _tail():
        _ln_rows(x_ref, g_ref, b_ref, (tail_ref,), rs)


def _ln_split_call(x, g, b, tr, m_head):
    m, d = x.shape
    n_head = m_head // tr
    return pl.pallas_call(
        functools.partial(_ln_split_kernel, rs=_pick_tile(tr, (32, 16)), n_head=n_head),
        grid=(m // tr,),
        in_specs=[pl.BlockSpec((tr, d), lambda i: (i, 0)),
                  pl.BlockSpec((1, d), lambda i: (0, 0)),
                  pl.BlockSpec((1, d), lambda i: (0, 0))],
        out_specs=[pl.BlockSpec((tr, d), lambda i: (jnp.minimum(i, n_head - 1), 0)),
                   pl.BlockSpec((tr, d), lambda i: (jnp.maximum(i - n_head, 0), 0))],
        out_shape=[jax.ShapeDtypeStruct((m_head, d), jnp.float32),
                   jax.ShapeDtypeStruct((m - m_head, d), jnp.float32)],
        compiler_params=_params(("arbitrary",)),
        name="layer_norm_out",
    )(x, g, b)


def _ln_call(x, g, b, tr):
    m, d = x.shape
    return pl.pallas_call(
        functools.partial(_ln_kernel, rs=_pick_tile(tr, (32, 16))),
        grid=(m // tr,),
        in_specs=[pl.BlockSpec((tr, d), lambda i: (i, 0)),
                  pl.BlockSpec((1, d), lambda i: (0, 0)),
                  pl.BlockSpec((1, d), lambda i: (0, 0))],
        out_specs=[pl.BlockSpec((tr, d), lambda i: (i, 0)),
                   pl.BlockSpec((tr, d), lambda i: (i, 0))],
        out_shape=[jax.ShapeDtypeStruct((m, d), jnp.float32),
                   jax.ShapeDtypeStruct((m, d), jnp.bfloat16)],
        compiler_params=_params(("parallel",)),
        name="layer_norm",
    )(x, g, b)


def kernel(x_prompt, x_sample, cache_k, cache_v, cache_kidx, state_conv, state_h, p_prompt, p_sample, w_in, conv_w, conv_b, gate_a_w, gate_a_b, gate_x_w, gate_x_b, lru_lambda, w_out, ln_g, ln_b, ple_proj, ple_gate):
    bf16 = jnp.bfloat16
    n_bp, t_p, d = x_prompt.shape
    n_bs, t_s, _ = x_sample.shape
    depth = w_in.shape[0]
    n_past = cache_k.shape[2]
    lru_w = conv_w.shape[2]
    att_w = d - lru_w
    n_heads = att_w // HEAD_DIM
    kv_w = N_KV_HEADS * HEAD_DIM
    qi_w = N_IDX_HEADS * IDX_DIM
    alpha = (2 * depth) ** 0.25
    m_p, m_s = n_bp * t_p, n_bs * t_s
    m = m_p + m_s
    assert t_p % KEY_BLOCK == 0 and m_p % t_s == 0 and t_s % SUBLANES == 0 and n_past % KEY_BLOCK == 0
    assert lru_w // N_LRU_BLOCKS == LANES and CHUNK == 1 << CHUNK_SHIFT

    o_u, o_gl = 0, lru_w
    o_q = 2 * lru_w
    o_k = o_q + att_w
    o_v = o_k + kv_w
    o_ga = o_v + kv_w
    o_qi = o_ga + att_w
    o_ki = o_qi + qi_w
    c_qi, c_q = 0, qi_w
    c_ga = c_q + att_w
    c_u = c_ga + att_w
    c_gl = c_u + lru_w
    c_k = c_gl + lru_w
    c_v = c_k + kv_w
    n_main = c_v + kv_w

    tm = _pick_tile(m, (768, 512, 384, 256, 128))
    tn_in = _pick_tile(n_main, (1024, 512))
    tn_out = _pick_tile(d, (512,))
    tr_ln = _pick_tile(m, (256, 128))
    tr_out = [c for c in (256, 128, 64, 32, 16, 8) if m_p % c == 0 and m_s % c == 0][0]
    ct_p = _pick_tile(lru_w, (256,))
    tq = _pick_tile(t_p, (256,))

    xf = jnp.concatenate([x_prompt.reshape(m_p, d), x_sample.reshape(m_s, d)], axis=0)
    xb = xf.astype(bf16)
    hist_pad = ((0, 0), (SUBLANES - (CONV_W - 1), 0), (0, 0))
    zero_conv = jnp.zeros((n_bp, SUBLANES, lru_w), jnp.float32)
    zero_h = jnp.zeros((n_bp, 1, lru_w), jnp.float32)
    past = (cache_k.reshape(depth, n_bs, n_past * N_KV_HEADS, HEAD_DIM),
            cache_v.reshape(depth, n_bs, n_past * N_KV_HEADS, HEAD_DIM), cache_kidx)

    outs_p, outs_s = [], []
    for i in range(depth):
        wl = w_in[i]
        seg = lambda o, n: wl[:, o:o + n]
        w_main = jnp.concatenate(
            [seg(o_qi, qi_w), seg(o_q, att_w), seg(o_ga, att_w), seg(o_u, lru_w), seg(o_gl, lru_w),
             seg(o_k, kv_w), seg(o_v, kv_w)], axis=1).astype(bf16)
        w_side = wl[:, o_ki:].astype(bf16)

        z = _matmul(xb, w_main, tm, tn_in)
        zs = _matmul(xb, w_side, tm, w_side.shape[1])
        ki = zs[:, :IDX_DIM]
        wi = zs[:, IDX_DIM:]
        wt_p = wi[:m_p].T
        wrow_s = wi[m_p:].reshape(n_bs, t_s, N_IDX_HEADS).transpose(0, 2, 1).reshape(n_bs, 1, -1)

        lru_args = (conv_w[i], conv_b[i][None], gate_a_w[i].astype(bf16), gate_a_b[i][None],
                    gate_x_w[i].astype(bf16), gate_x_b[i][None], lru_lambda[i][None])
        yl, conv_p, h_p = _lru_call(z, 0, n_bp, t_p, ct_p, c_u, c_gl, zero_conv, zero_h, *lru_args,
                                    pos0=0, y_rows=m)
        yl, conv_s, h_s = _lru_call(z, m_p // t_s, n_bs, t_s, lru_w, c_u, c_gl,
                                    jnp.pad(state_conv[i], hist_pad), state_h[i][:, None, :], *lru_args,
                                    pos0=n_past, y_rows=m, y_prev=yl)
        conv_p, conv_s = conv_p[:, :CONV_W - 1], conv_s[:, :CONV_W - 1]

        cols = (c_qi, c_q, c_ga, c_k, c_v)
        ya = _attn_call(z, ki, wt_p, n_b=n_bp, t_len=t_p, tq=tq, cols=cols, y_rows=m)
        ya = _decode_attn_call(z, ki, wrow_s, past, ya, layer=i, row_blk0=m_p // t_s, n_b=n_bs, t_q=t_s,
                               pos0=n_past, cols=cols)

        p = jnp.concatenate([p_prompt[i].reshape(m_p, -1), p_sample[i].reshape(m_s, -1)], axis=0).astype(bf16)
        pre = _out_call(yl, ya, xb, p, xf, w_out[i].astype(bf16), ple_gate[i].astype(bf16),
                        ple_proj[i].astype(bf16), alpha, tm, tn_out)
        if i + 1 < depth:
            xf, xb = _ln_call(pre, ln_g[i][None], ln_b[i][None], tr_ln)
        else:
            y_p, y_s = _ln_split_call(pre, ln_g[i][None], ln_b[i][None], tr_out, m_p)

        outs_p.append((z[:m_p, c_k:c_k + kv_w].reshape(n_bp, t_p, N_KV_HEADS, HEAD_DIM),
                       z[:m_p, c_v:c_v + kv_w].reshape(n_bp, t_p, N_KV_HEADS, HEAD_DIM),
                       ki[:m_p].reshape(n_bp, t_p, IDX_DIM), conv_p, h_p[:, 0, :]))
        outs_s.append((z[m_p:, c_k:c_k + kv_w].reshape(n_bs, t_s, N_KV_HEADS, HEAD_DIM),
                       z[m_p:, c_v:c_v + kv_w].reshape(n_bs, t_s, N_KV_HEADS, HEAD_DIM),
                       ki[m_p:].reshape(n_bs, t_s, IDX_DIM), conv_s, h_s[:, 0, :]))

    stack = lambda outs, j: jnp.stack([o[j] for o in outs])
    return (y_p.reshape(n_bp, t_p, d), y_s.reshape(n_bs, t_s, d),
            stack(outs_p, 0), stack(outs_p, 1), stack(outs_p, 2), stack(outs_p, 3), stack(outs_p, 4),
            stack(outs_s, 0), stack(outs_s, 1), stack(outs_s, 2), stack(outs_s, 3), stack(outs_s, 4))
```

```python
import functools

import jax
import jax.numpy as jnp
from jax import lax
from jax.experimental import pallas as pl
from jax.experimental.pallas import tpu as pltpu

CHUNK = 64
CHUNK_SHIFT = 6
N_LRU_BLOCKS = 16
CONV_W = 4
LRU_C = 8.0
HEAD_DIM = 128
N_KV_HEADS = 4
N_IDX_HEADS = 32
IDX_DIM = 128
TOPK_MAX = 256
LN_EPS = 1e-5
LOG2_E = 1.4426950408889634

LANES = 128
SUBLANES = 8
VMEM_LIMIT_BYTES = 56 * 1024 * 1024

KEY_BLOCK = 256
KEY_BLOCK_SHIFT = 8

INT_MIN = -(2 ** 31)
KEY_NEG_INF = -2139095041

_NT = (((1,), (1,)), ((), ()))


def _pick_tile(n, candidates):
    for c in candidates:
        if n % c == 0:
            return c
    return n


def _params(sem):
    return pltpu.CompilerParams(dimension_semantics=sem, vmem_limit_bytes=VMEM_LIMIT_BYTES)


def _cast_kernel(w_ref, o_ref, *, rs):
    def piece(c, carry):
        rows = pl.ds(pl.multiple_of(c * rs, rs), rs)
        o_ref[rows, :] = w_ref[rows, :].astype(o_ref.dtype)
        return carry

    lax.fori_loop(0, w_ref.shape[0] // rs, piece, 0)


def _cast_call(w, rows):
    depth, k, n = w.shape
    spec = pl.BlockSpec((None, rows, n), lambda l, i: (l, i, 0))
    return pl.pallas_call(
        functools.partial(_cast_kernel, rs=_pick_tile(rows, (32, 16))),
        grid=(depth, k // rows),
        in_specs=[spec],
        out_specs=spec,
        out_shape=jax.ShapeDtypeStruct(w.shape, jnp.bfloat16),
        compiler_params=_params(("parallel", "parallel")),
        name="cast_bf16",
    )(w)


def _in_proj_kernel(x_ref, w_ref, z_ref, k_ref, v_ref, *, j_kv):
    z_ref[...] = jnp.dot(x_ref[...], w_ref[...], preferred_element_type=jnp.float32)

    @pl.when(pl.program_id(1) == j_kv)
    def _emit_kv():
        tm = z_ref.shape[0]
        kv_w = N_KV_HEADS * HEAD_DIM
        for g in range(N_KV_HEADS):
            rows = pl.ds(g, tm, stride=N_KV_HEADS)
            k_ref[rows, :] = z_ref[:, g * HEAD_DIM:(g + 1) * HEAD_DIM]
            v_ref[rows, :] = z_ref[:, kv_w + g * HEAD_DIM:kv_w + (g + 1) * HEAD_DIM]


def _in_proj_call(x, w, tm, tn, k_col):
    m, kdim = x.shape
    n = w.shape[1]
    assert k_col % tn == 0 and tn == 2 * N_KV_HEADS * HEAD_DIM
    kv_shape = jax.ShapeDtypeStruct((m * N_KV_HEADS, HEAD_DIM), jnp.float32)
    kv_spec = pl.BlockSpec((tm * N_KV_HEADS, HEAD_DIM), lambda i, j: (i, 0))
    return pl.pallas_call(
        functools.partial(_in_proj_kernel, j_kv=k_col // tn),
        grid=(m // tm, n // tn),
        in_specs=[pl.BlockSpec((tm, kdim), lambda i, j: (i, 0)),
                  pl.BlockSpec((kdim, tn), lambda i, j: (0, j))],
        out_specs=[pl.BlockSpec((tm, tn), lambda i, j: (i, j)), kv_spec, kv_spec],
        out_shape=[jax.ShapeDtypeStruct((m, n), jnp.float32), kv_shape, kv_shape],
        compiler_params=_params(("parallel", "arbitrary")),
        name="in_proj",
    )(x, w)


def _idx_proj_kernel(x_ref, w_ref, ki_ref, wi_ref):
    r = jnp.dot(x_ref[...], w_ref[...], preferred_element_type=jnp.float32)
    ki_ref[...] = r[:, 0:IDX_DIM]
    wi_ref[...] = r[:, IDX_DIM:]


def _idx_proj_call(x, w, tm):
    m, kdim = x.shape
    n = w.shape[1]
    return pl.pallas_call(
        _idx_proj_kernel,
        grid=(m // tm,),
        in_specs=[pl.BlockSpec((tm, kdim), lambda i: (i, 0)),
                  pl.BlockSpec((kdim, n), lambda i: (0, 0))],
        out_specs=[pl.BlockSpec((tm, IDX_DIM), lambda i: (i, 0)),
                   pl.BlockSpec((tm, n - IDX_DIM), lambda i: (i, 0))],
        out_shape=[jax.ShapeDtypeStruct((m, IDX_DIM), jnp.float32),
                   jax.ShapeDtypeStruct((m, n - IDX_DIM), jnp.float32)],
        compiler_params=_params(("parallel",)),
        name="idx_proj",
    )(x, w)


def _sigmoid(x):
    return 0.5 * jnp.tanh(0.5 * x) + 0.5


def _lru_kernel(u_ref, g_ref, cbuf_ref, h0_ref, cw_ref, cb_ref, gaw_ref, gab_ref,
                gxw_ref, gxb_ref, lam_ref, y_ref, cnew_ref, hlast_ref,
                full_ref, *, t_len, ct, rc, pos0):
    full_ref[0:SUBLANES, :] = cbuf_ref[...]
    full_ref[SUBLANES:SUBLANES + t_len, :] = u_ref[...]
    cnew_ref[...] = pltpu.roll(full_ref[t_len:t_len + SUBLANES, :], CONV_W - 1, 0)

    n_tiles = rc // SUBLANES
    row = lax.broadcasted_iota(jnp.int32, (SUBLANES, ct), 0)
    cw = cw_ref[...]
    cb = cb_ref[...]
    nlam = -lam_ref[...]
    softplus = jnp.maximum(nlam, 0.0) + jnp.log1p(jnp.exp(-jnp.abs(nlam)))
    gab = gab_ref[...]
    gxb = gxb_ref[...]

    def chunk(c, h):
        r0 = pl.multiple_of(c * rc, rc)
        conv_tiles = []
        for t in range(n_tiles):
            prev = full_ref[pl.ds(r0 + t * SUBLANES, SUBLANES), :]
            cur = full_ref[pl.ds(r0 + (t + 1) * SUBLANES, SUBLANES), :]
            acc = cb + cur * cw[CONV_W - 1:CONV_W, :]
            for s in range(1, CONV_W):
                shifted = pltpu.roll(jnp.where(row >= SUBLANES - s, prev, cur), s, 0)
                acc = acc + shifted * cw[CONV_W - 1 - s:CONV_W - s, :]
            conv_tiles.append(acc)
        conv = jnp.concatenate(conv_tiles, axis=0) if n_tiles > 1 else conv_tiles[0]
        conv16 = conv.astype(jnp.bfloat16)
        r_parts, i_parts = [], []
        for n in range(ct // LANES):
            ub = conv16[:, n * LANES:(n + 1) * LANES]
            r_parts.append(jnp.dot(ub, gaw_ref[n], preferred_element_type=jnp.float32))
            i_parts.append(jnp.dot(ub, gxw_ref[n], preferred_element_type=jnp.float32))
        r = _sigmoid(jnp.concatenate(r_parts, axis=1) + gab)
        gi = _sigmoid(jnp.concatenate(i_parts, axis=1) + gxb)
        log_a = -LRU_C * r * softplus
        th = jnp.tanh(log_a)
        mult = jnp.sqrt(-2.0 * th / (1.0 - th))
        a = jnp.exp(log_a)
        b = conv * gi * mult

        scanned = []
        for t in range(n_tiles):
            ac = a[t * SUBLANES:(t + 1) * SUBLANES, :]
            bc = b[t * SUBLANES:(t + 1) * SUBLANES, :]
            if t == 0 and pos0 == 0:
                start = (row == 0) & (r0 == 0)
                bc = jnp.where(start, conv[0:SUBLANES, :] * gi[0:SUBLANES, :], bc)
            for s in (1, 2, 4):
                a_sh = jnp.where(row >= s, pltpu.roll(ac, s, 0), 1.0)
                b_sh = jnp.where(row >= s, pltpu.roll(bc, s, 0), 0.0)
                bc = ac * b_sh + bc
                ac = ac * a_sh
            scanned.append((ac, bc))
        h_rows = []
        for ac, bc in scanned:
            h_rows.append(ac * h + bc)
            h = ac[SUBLANES - 1:SUBLANES, :] * h + bc[SUBLANES - 1:SUBLANES, :]
        h_seq = jnp.concatenate(h_rows, axis=0) if n_tiles > 1 else h_rows[0]
        g = g_ref[pl.ds(r0, rc), :]
        y_ref[pl.ds(r0, rc), :] = (h_seq * (g * _sigmoid(g))).astype(y_ref.dtype)
        return h

    hlast_ref[...] = lax.fori_loop(0, t_len // rc, chunk, h0_ref[...])


def _drop_ref(fn, pos):
    def body(*refs):
        return fn(*refs[:pos], *refs[pos + 1:])
    return body


def _lru_call(z, row_blk0, n_b, t_len, ct, u_col, g_col, cbuf, h0, cw, cb, gaw, gab, gxw, gxb, lam, pos0,
              y_rows, y_prev=None):
    lru_w = cw.shape[1]
    n_c = lru_w // ct
    nb_blk = ct // LANES
    rc = _pick_tile(t_len, (128, 64, 32, 16))
    kern = functools.partial(_lru_kernel, t_len=t_len, ct=ct, rc=rc, pos0=pos0)
    vec = lambda: pl.BlockSpec((1, ct), lambda b, c: (0, c))
    args = [z, z, cbuf, h0, cw, cb, gaw, gab, gxw, gxb, lam]
    extra_specs, aliases = [], {}
    if y_prev is not None:
        kern = _drop_ref(kern, len(args))
        extra_specs = [pl.BlockSpec(memory_space=pl.ANY)]
        aliases = {len(args): 0}
        args.append(y_prev)
    return pl.pallas_call(
        kern,
        grid=(n_b, n_c),
        input_output_aliases=aliases,
        in_specs=[
            pl.BlockSpec((t_len, ct), lambda b, c: (row_blk0 + b, u_col // ct + c)),
            pl.BlockSpec((t_len, ct), lambda b, c: (row_blk0 + b, g_col // ct + c)),
            pl.BlockSpec((None, SUBLANES, ct), lambda b, c: (b, 0, c)),
            pl.BlockSpec((None, 1, ct), lambda b, c: (b, 0, c)),
            pl.BlockSpec((CONV_W, ct), lambda b, c: (0, c)),
            vec(),
            pl.BlockSpec((nb_blk, LANES, LANES), lambda b, c: (c, 0, 0)),
            vec(),
            pl.BlockSpec((nb_blk, LANES, LANES), lambda b, c: (c, 0, 0)),
            vec(),
            vec(),
        ] + extra_specs,
        out_specs=[
            pl.BlockSpec((t_len, ct), lambda b, c: (row_blk0 + b, c)),
            pl.BlockSpec((None, SUBLANES, ct), lambda b, c: (b, 0, c)),
            pl.BlockSpec((None, 1, ct), lambda b, c: (b, 0, c)),
        ],
        out_shape=[
            jax.ShapeDtypeStruct((y_rows, lru_w), jnp.bfloat16),
            jax.ShapeDtypeStruct((n_b, SUBLANES, lru_w), jnp.float32),
            jax.ShapeDtypeStruct((n_b, 1, lru_w), jnp.float32),
        ],
        scratch_shapes=[
            pltpu.VMEM((t_len + SUBLANES, ct), jnp.float32),
        ],
        compiler_params=_params(("parallel", "parallel")),
        name="rg_lru",
    )(*args)


def _tree(x, op):
    tiles = [x[i:i + SUBLANES] for i in range(0, x.shape[0], SUBLANES)]
    while len(tiles) > 1:
        pairs = [op(tiles[i], tiles[i + 1]) for i in range(0, len(tiles) - 1, 2)]
        tiles = pairs + tiles[len(pairs) * 2:]
    return tiles[0]


def _key_to_float(key):
    return pltpu.bitcast(jnp.where(key < 0, key ^ 0x7FFFFFFF, key), jnp.float32)


def _kth_largest(count_ge, k, lanes):
    thr = jnp.where(count_ge(jnp.zeros((1, lanes), jnp.float32)) >= k, 0, INT_MIN).astype(jnp.int32)

    def bit_body(i, t):
        cand = t + lax.shift_left(jnp.int32(1), 30 - i)
        return jnp.where(count_ge(_key_to_float(cand)) >= k, cand, t)

    thr = lax.fori_loop(0, 31, bit_body, thr)
    return _key_to_float(jnp.maximum(thr, KEY_NEG_INF + 1))


def _n_key_chunks(last_q, s_all):
    n_allowed = jnp.minimum(
        lax.shift_left(lax.shift_right_logical(last_q, CHUNK_SHIFT) + 1, CHUNK_SHIFT), s_all)
    return lax.shift_right_logical(n_allowed + (KEY_BLOCK - 1), KEY_BLOCK_SHIFT)


def _attn_kernel(qi_ref, q_ref, ga_ref, wt_ref, k_ref, v_ref, ki_ref, y_ref,
                 kb_ref, vt_ref, kib_ref, qis_ref, qs_ref, sc_ref, m_ref, s_ref, ot_ref,
                 *, tq, n_sel, n_heads):
    kb = KEY_BLOCK
    group = n_heads // N_KV_HEADS
    s_all = sc_ref.shape[0]
    qb = pl.program_id(1)

    @pl.when(qb == 0)
    def _assemble_keys():
        for blk in range(s_all // LANES):
            rows = slice(blk * LANES, (blk + 1) * LANES)
            kib_ref[rows, :] = ki_ref[rows, :].astype(jnp.bfloat16)
            for g in range(N_KV_HEADS):
                cols = slice(g * HEAD_DIM, (g + 1) * HEAD_DIM)
                kb_ref[g, rows, :] = k_ref[rows, cols].astype(jnp.bfloat16)
                lane0 = (blk * LANES) % kb
                vt_ref[g, (blk * LANES) // kb, :, lane0:lane0 + LANES] = (
                    v_ref[rows, cols].T.astype(jnp.bfloat16))

    for h in range(N_IDX_HEADS):
        qis_ref[h * tq:(h + 1) * tq, :] = qi_ref[:, h * IDX_DIM:(h + 1) * IDX_DIM].astype(jnp.bfloat16)
    q_scale = LOG2_E * HEAD_DIM ** -0.5
    for h in range(n_heads):
        qs_ref[h * tq:(h + 1) * tq, :] = (
            q_ref[:, h * HEAD_DIM:(h + 1) * HEAD_DIM] * q_scale).astype(jnp.bfloat16)

    w = wt_ref[...] * (N_IDX_HEADS ** -0.5)
    qpos = qb * tq + lax.broadcasted_iota(jnp.int32, (1, tq), 1)
    qchunk = lax.shift_right_logical(qpos, CHUNK_SHIFT)
    n_chunks = _n_key_chunks(qb * tq + (tq - 1), s_all)

    def chunk_rows(c):
        return pl.ds(pl.multiple_of(c * kb, kb), kb)

    def score_chunk(c, carry):
        rows = chunk_rows(c)
        kic = kib_ref[rows, :]
        sc = jnp.zeros((kb, tq), jnp.float32)
        for hg in range(N_IDX_HEADS // 4):
            rel = lax.dot_general(kic, qis_ref[hg * 4 * tq:(hg + 1) * 4 * tq, :], _NT,
                                  preferred_element_type=jnp.float32)
            for j in range(4):
                h = hg * 4 + j
                sc = sc + jnp.maximum(rel[:, j * tq:(j + 1) * tq], 0.0) * w[h:h + 1, :]
        sc = sc * (IDX_DIM ** -0.5)
        kpos = c * kb + lax.broadcasted_iota(jnp.int32, (kb, tq), 0)
        allowed = lax.shift_right_logical(kpos, CHUNK_SHIFT) <= qchunk
        sc_ref[rows, :] = jnp.where(allowed, sc, -jnp.inf)
        return carry

    lax.fori_loop(0, n_chunks, score_chunk, 0)

    def count_ge(cand_f):
        def body(c, acc):
            return acc + _tree(jnp.where(sc_ref[chunk_rows(c), :] >= cand_f, 1.0, 0.0), jnp.add)
        acc = lax.fori_loop(0, n_chunks, body, jnp.zeros((SUBLANES, tq), jnp.float32))
        return jnp.sum(acc, axis=0, keepdims=True)

    thr_f = _kth_largest(count_ge, n_sel, tq)

    m_ref[...] = jnp.full(m_ref.shape, -jnp.inf, jnp.float32)
    s_ref[...] = jnp.zeros(s_ref.shape, jnp.float32)
    ot_ref[...] = jnp.zeros(ot_ref.shape, jnp.float32)
    slopes2 = [LOG2_E * 2.0 ** (-8.0 * (h + 1) / n_heads) for h in range(n_heads)]

    def attn_chunk(c, carry):
        rows = chunk_rows(c)
        kpos = c * kb + lax.broadcasted_iota(jnp.int32, (kb, tq), 0)
        dist = jnp.abs(qpos - kpos).astype(jnp.float32)
        dmc = jnp.where(sc_ref[rows, :] >= thr_f, dist, jnp.inf)
        for g in range(N_KV_HEADS):
            qg = qs_ref[g * group * tq:(g + 1) * group * tq, :]
            lg = lax.dot_general(kb_ref[g, rows, :], qg, _NT, preferred_element_type=jnp.float32)
            ps, alphas = [], []
            for j in range(group):
                h = g * group + j
                lj = lg[:, j * tq:(j + 1) * tq] - slopes2[h] * dmc
                m_old = m_ref[h:h + 1, :]
                m_new = jnp.maximum(m_old, jnp.max(_tree(lj, jnp.maximum), axis=0, keepdims=True))
                m_safe = jnp.where(m_new == -jnp.inf, 0.0, m_new)
                alpha = jnp.exp2(m_old - m_safe)
                p = jnp.exp2(lj - m_safe)
                srows = slice(h * SUBLANES, (h + 1) * SUBLANES)
                s_ref[srows, :] = s_ref[srows, :] * alpha + _tree(p, jnp.add)
                m_ref[h:h + 1, :] = m_new
                ps.append(p.astype(jnp.bfloat16))
                alphas.append(alpha)
            pv = jnp.dot(vt_ref[g, c], jnp.concatenate(ps, axis=1), preferred_element_type=jnp.float32)
            ot_ref[g] = ot_ref[g] * jnp.concatenate(alphas, axis=1) + pv
        return carry

    lax.fori_loop(0, n_chunks, attn_chunk, 0)

    for h in range(n_heads):
        g, j = divmod(h, group)
        s = jnp.sum(s_ref[h * SUBLANES:(h + 1) * SUBLANES, :], axis=0, keepdims=True)
        o = (ot_ref[g, :, j * tq:(j + 1) * tq] / s).T
        gate = ga_ref[:, h * HEAD_DIM:(h + 1) * HEAD_DIM]
        y_ref[:, h * HEAD_DIM:(h + 1) * HEAD_DIM] = (o * (gate * _sigmoid(gate))).astype(y_ref.dtype)


def _attn_call(z, ki, wt, y_prev, *, n_b, t_len, tq, cols):
    qi_col, q_col, ga_col, k_col, v_col = cols
    n_heads = (ga_col - q_col) // HEAD_DIM
    att_w = n_heads * HEAD_DIM
    qi_w = N_IDX_HEADS * IDX_DIM
    kv_w = N_KV_HEADS * HEAD_DIM
    n_q = t_len // tq
    n_sel = min(TOPK_MAX, t_len // 4)
    kern = functools.partial(_attn_kernel, tq=tq, n_sel=n_sel, n_heads=n_heads)
    qrow = lambda b, q: b * n_q + q
    n_args = 7
    return pl.pallas_call(
        _drop_ref(kern, n_args),
        grid=(n_b, n_q),
        input_output_aliases={n_args: 0},
        in_specs=[
            pl.BlockSpec((tq, qi_w), lambda b, q: (qrow(b, q), qi_col // qi_w)),
            pl.BlockSpec((tq, att_w), lambda b, q: (qrow(b, q), q_col // att_w)),
            pl.BlockSpec((tq, att_w), lambda b, q: (qrow(b, q), ga_col // att_w)),
            pl.BlockSpec((N_IDX_HEADS, tq), lambda b, q: (0, qrow(b, q))),
            pl.BlockSpec((t_len, kv_w), lambda b, q: (b, k_col // kv_w)),
            pl.BlockSpec((t_len, kv_w), lambda b, q: (b, v_col // kv_w)),
            pl.BlockSpec((t_len, IDX_DIM), lambda b, q: (b, 0)),
            pl.BlockSpec(memory_space=pl.ANY),
        ],
        out_specs=pl.BlockSpec((tq, att_w), lambda b, q: (qrow(b, q), 0)),
        out_shape=jax.ShapeDtypeStruct(y_prev.shape, y_prev.dtype),
        scratch_shapes=[
            pltpu.VMEM((N_KV_HEADS, t_len, HEAD_DIM), jnp.bfloat16),
            pltpu.VMEM((N_KV_HEADS, t_len // KEY_BLOCK, HEAD_DIM, KEY_BLOCK), jnp.bfloat16),
            pltpu.VMEM((t_len, IDX_DIM), jnp.bfloat16),
            pltpu.VMEM((N_IDX_HEADS * tq, IDX_DIM), jnp.bfloat16),
            pltpu.VMEM((n_heads * tq, HEAD_DIM), jnp.bfloat16),
            pltpu.VMEM((t_len, tq), jnp.float32),
            pltpu.VMEM((n_heads, tq), jnp.float32),
            pltpu.VMEM((n_heads * SUBLANES, tq), jnp.float32),
            pltpu.VMEM((N_KV_HEADS, HEAD_DIM, (n_heads // N_KV_HEADS) * tq), jnp.float32),
        ],
        compiler_params=_params(("arbitrary", "arbitrary")),
        name="sparse_attn",
    )(z, z, z, wt, z, z, ki, y_prev)


def _decode_attn_kernel(qi_ref, q_ref, ga_ref, wrow_ref, kcur_ref, vcur_ref, kicur_ref,
                        kp_ref, vp_ref, kip_ref, y_ref,
                        kb_ref, vt_ref, kib_ref, qit_ref, bdt_ref, sc_ref, stage_ref,
                        *, t_q, n_past, pos0, n_sel, n_heads):
    kb = KEY_BLOCK
    group = n_heads // N_KV_HEADS
    hpt = LANES // t_q
    n_tiles = n_heads // hpt
    gpt = hpt // group
    s_all = n_past + t_q
    s_pad = sc_ref.shape[0]
    kv_w = N_KV_HEADS * HEAD_DIM
    tile_w = gpt * HEAD_DIM
    slot_shifts = [LANES >> (i + 1) for i in range(hpt.bit_length() - 1)]

    def sum_over_slots(x):
        for sh in slot_shifts:
            x = x + pltpu.roll(x, sh, 1)
        return x

    def load_block(k_src, v_src, ki_src, src_row, dst_row, heads_on_rows):
        rows = slice(src_row, src_row + LANES)
        dst = slice(dst_row, dst_row + LANES)
        kib_ref[dst, :] = ki_src[rows, 0:IDX_DIM].astype(jnp.bfloat16)
        for g in range(N_KV_HEADS):
            if heads_on_rows:
                idx = (pl.ds(src_row * N_KV_HEADS + g, LANES, stride=N_KV_HEADS), slice(None))
            else:
                idx = (rows, slice(g * HEAD_DIM, (g + 1) * HEAD_DIM))
            kb_ref[dst, g * HEAD_DIM:(g + 1) * HEAD_DIM] = k_src[idx].astype(jnp.bfloat16)
            lane0 = dst_row % kb
            vt_ref[g // gpt, dst_row // kb, (g % gpt) * HEAD_DIM:(g % gpt + 1) * HEAD_DIM,
                   lane0:lane0 + LANES] = v_src[idx].T.astype(jnp.bfloat16)

    for blk in range(n_past // LANES):
        load_block(kp_ref, vp_ref, kip_ref, blk * LANES, blk * LANES, True)
    stage_ref[...] = jnp.zeros(stage_ref.shape, jnp.float32)
    stage_ref[0, 0:t_q, 0:kv_w] = kcur_ref[...]
    stage_ref[1, 0:t_q, 0:kv_w] = vcur_ref[...]
    stage_ref[2, 0:t_q, 0:IDX_DIM] = kicur_ref[...]
    for blk in range((s_pad - n_past) // LANES):
        load_block(stage_ref.at[0], stage_ref.at[1], stage_ref.at[2], blk * LANES, n_past + blk * LANES,
                   False)

    for h in range(N_IDX_HEADS):
        qit_ref[h * t_q:(h + 1) * t_q, :] = qi_ref[:, h * IDX_DIM:(h + 1) * IDX_DIM].astype(jnp.bfloat16)
    bdt_ref[...] = jnp.zeros(bdt_ref.shape, bdt_ref.dtype)
    q_scale = LOG2_E * HEAD_DIM ** -0.5
    for h in range(n_heads):
        tile, slot = divmod(h, hpt)
        gl = slot // group
        bdt_ref[tile, slot * t_q:(slot + 1) * t_q, gl * HEAD_DIM:(gl + 1) * HEAD_DIM] = (
            q_ref[:, h * HEAD_DIM:(h + 1) * HEAD_DIM] * q_scale).astype(jnp.bfloat16)

    lane = lax.broadcasted_iota(jnp.int32, (1, LANES), 1)
    slot_l = lax.shift_right_logical(lane, t_q.bit_length() - 1)
    qpos = pos0 + (lane & (t_q - 1))
    qchunk = lax.shift_right_logical(qpos, CHUNK_SHIFT)
    w = wrow_ref[...] * (N_IDX_HEADS ** -0.5)
    n_chunks = s_pad // kb

    for c in range(n_chunks):
        rows = slice(c * kb, (c + 1) * kb)
        rel = lax.dot_general(kib_ref[rows, :], qit_ref[...], _NT, preferred_element_type=jnp.float32)
        x = jnp.maximum(rel, 0.0) * w
        acc = x[:, 0:LANES]
        for i in range(1, x.shape[1] // LANES):
            acc = acc + x[:, i * LANES:(i + 1) * LANES]
        sc = sum_over_slots(acc) * (IDX_DIM ** -0.5)
        kpos = c * kb + lax.broadcasted_iota(jnp.int32, (kb, LANES), 0)
        allowed = (lax.shift_right_logical(kpos, CHUNK_SHIFT) <= qchunk) & (kpos < s_all)
        sc_ref[rows, :] = jnp.where(allowed, sc, -jnp.inf)

    def count_ge(cand_f):
        return jnp.sum(_tree(jnp.where(sc_ref[...] >= cand_f, 1.0, 0.0), jnp.add), axis=0, keepdims=True)

    thr_f = _kth_largest(count_ge, n_sel, LANES)

    slopes2 = []
    for p in range(n_tiles):
        head = (p * hpt + slot_l + 1).astype(jnp.float32)
        slopes2.append(LOG2_E * jnp.exp2(head * (-8.0 / n_heads)))
    m = [jnp.full((1, LANES), -jnp.inf, jnp.float32) for _ in range(n_tiles)]
    s8 = [jnp.zeros((SUBLANES, LANES), jnp.float32) for _ in range(n_tiles)]
    ot = [jnp.zeros((tile_w, LANES), jnp.float32) for _ in range(n_tiles)]
    for c in range(n_chunks):
        rows = slice(c * kb, (c + 1) * kb)
        kpos = c * kb + lax.broadcasted_iota(jnp.int32, (kb, LANES), 0)
        dist = jnp.abs(qpos - kpos).astype(jnp.float32)
        dmc = jnp.where(sc_ref[rows, :] >= thr_f, dist, jnp.inf)
        for p in range(n_tiles):
            lg = lax.dot_general(kb_ref[rows, p * tile_w:(p + 1) * tile_w], bdt_ref[p], _NT,
                                 preferred_element_type=jnp.float32)
            lj = lg - slopes2[p] * dmc
            m_new = jnp.maximum(m[p], jnp.max(_tree(lj, jnp.maximum), axis=0, keepdims=True))
            m_safe = jnp.where(m_new == -jnp.inf, 0.0, m_new)
            alpha = jnp.exp2(m[p] - m_safe)
            pr = jnp.exp2(lj - m_safe)
            s8[p] = s8[p] * alpha + _tree(pr, jnp.add)
            m[p] = m_new
            pv = jnp.dot(vt_ref[p, c], pr.astype(jnp.bfloat16), preferred_element_type=jnp.float32)
            ot[p] = ot[p] * alpha + pv

    for p in range(n_tiles):
        o = ot[p] / jnp.sum(s8[p], axis=0, keepdims=True)
        for gl in range(gpt):
            o_t = o[gl * HEAD_DIM:(gl + 1) * HEAD_DIM, :].T
            for j in range(group):
                slot = gl * group + j
                h = p * hpt + slot
                gate = ga_ref[:, h * HEAD_DIM:(h + 1) * HEAD_DIM]
                y_ref[:, h * HEAD_DIM:(h + 1) * HEAD_DIM] = (
                    o_t[slot * t_q:(slot + 1) * t_q, :] * (gate * _sigmoid(gate))).astype(y_ref.dtype)


def _decode_attn_call(z, ki, wrow, past, y_prev, *, layer, row_blk0, n_b, t_q, pos0, cols):
    qi_col, q_col, ga_col, k_col, v_col = cols
    n_heads = (ga_col - q_col) // HEAD_DIM
    att_w = n_heads * HEAD_DIM
    qi_w = N_IDX_HEADS * IDX_DIM
    kv_w = N_KV_HEADS * HEAD_DIM
    group = n_heads // N_KV_HEADS
    hpt = LANES // t_q
    assert t_q & (t_q - 1) == 0 and LANES % t_q == 0 and n_heads % hpt == 0 and hpt % group == 0
    assert (N_IDX_HEADS * t_q) % LANES == 0
    n_past = past[2].shape[2]
    s_all = n_past + t_q
    cur_pad = ((t_q + KEY_BLOCK - 1) // KEY_BLOCK) * KEY_BLOCK
    s_pad = n_past + cur_pad
    n_sel = min(TOPK_MAX, s_all // 4)
    n_tiles = n_heads // hpt
    gpt = hpt // group
    kern = functools.partial(_decode_attn_kernel, t_q=t_q, n_past=n_past, pos0=pos0, n_sel=n_sel,
                             n_heads=n_heads)
    row = lambda b: row_blk0 + b
    n_args = 10
    return pl.pallas_call(
        _drop_ref(kern, n_args),
        grid=(n_b,),
        input_output_aliases={n_args: 0},
        in_specs=[
            pl.BlockSpec((t_q, qi_w), lambda b: (row(b), qi_col // qi_w)),
            pl.BlockSpec((t_q, att_w), lambda b: (row(b), q_col // att_w)),
            pl.BlockSpec((t_q, att_w), lambda b: (row(b), ga_col // att_w)),
            pl.BlockSpec((None, 1, N_IDX_HEADS * t_q), lambda b: (b, 0, 0)),
            pl.BlockSpec((t_q, kv_w), lambda b: (row(b), k_col // kv_w)),
            pl.BlockSpec((t_q, kv_w), lambda b: (row(b), v_col // kv_w)),
            pl.BlockSpec((t_q, IDX_DIM), lambda b: (row(b), 0)),
            pl.BlockSpec((None, None, n_past * N_KV_HEADS, HEAD_DIM), lambda b: (layer, b, 0, 0)),
            pl.BlockSpec((None, None, n_past * N_KV_HEADS, HEAD_DIM), lambda b: (layer, b, 0, 0)),
            pl.BlockSpec((None, None, n_past, IDX_DIM), lambda b: (layer, b, 0, 0)),
            pl.BlockSpec(memory_space=pl.ANY),
        ],
        out_specs=pl.BlockSpec((t_q, att_w), lambda b: (row(b), 0)),
        out_shape=jax.ShapeDtypeStruct(y_prev.shape, y_prev.dtype),
        scratch_shapes=[
            pltpu.VMEM((s_pad, kv_w), jnp.bfloat16),
            pltpu.VMEM((N_KV_HEADS // gpt, s_pad // KEY_BLOCK, gpt * HEAD_DIM, KEY_BLOCK), jnp.bfloat16),
            pltpu.VMEM((s_pad, IDX_DIM), jnp.bfloat16),
            pltpu.VMEM((N_IDX_HEADS * t_q, IDX_DIM), jnp.bfloat16),
            pltpu.VMEM((n_tiles, LANES, gpt * HEAD_DIM), jnp.bfloat16),
            pltpu.VMEM((s_pad, LANES), jnp.float32),
            pltpu.VMEM((3, cur_pad, kv_w), jnp.float32),
        ],
        compiler_params=_params(("parallel",)),
        name="decode_attn",
    )(z, z, z, wrow, z, z, ki, *past, y_prev)


def _out_kernel(yl_ref, ya_ref, xb_ref, p_ref, xf_ref, wo_ref, wg_ref, wp_ref, o_ref, *, alpha):
    half = yl_ref.shape[1]
    mix = (jnp.dot(yl_ref[...], wo_ref[0:half, :], preferred_element_type=jnp.float32)
           + jnp.dot(ya_ref[...], wo_ref[half:, :], preferred_element_type=jnp.float32))
    gate = jax.nn.sigmoid(jnp.dot(xb_ref[...], wg_ref[...], preferred_element_type=jnp.float32))
    emb = jnp.dot(p_ref[...], wp_ref[...], preferred_element_type=jnp.float32)
    o_ref[...] = alpha * xf_ref[...] + mix + gate * emb


def _out_call(yl, ya, xb, p, xf, wo, wg, wp, layer, alpha, tm, tn):
    m, d = xf.shape
    half = yl.shape[1]
    ple = p.shape[1]
    row = lambda w: pl.BlockSpec((tm, w), lambda i, j: (i, 0))
    col = lambda k: pl.BlockSpec((k, tn), lambda i, j: (0, j))
    lcol = lambda k: pl.BlockSpec((None, k, tn), lambda i, j: (layer, 0, j))
    return pl.pallas_call(
        functools.partial(_out_kernel, alpha=alpha),
        grid=(m // tm, d // tn),
        in_specs=[row(half), row(ya.shape[1]), row(d), row(ple),
                  pl.BlockSpec((tm, tn), lambda i, j: (i, j)),
                  lcol(wo.shape[1]), lcol(d), col(ple)],
        out_specs=pl.BlockSpec((tm, tn), lambda i, j: (i, j)),
        out_shape=jax.ShapeDtypeStruct((m, d), jnp.float32),
        compiler_params=_params(("parallel", "parallel")),
        name="out_proj",
    )(yl, ya, xb, p, xf, wo, wg, wp)


def _ln_rows(x_ref, g_ref, b_ref, out_refs, rs):
    def sub_block(c, carry):
        rows = pl.ds(pl.multiple_of(c * rs, rs), rs)
        x = x_ref[rows, :]
        mu = jnp.mean(x, axis=-1, keepdims=True)
        xc = x - mu
        var = jnp.mean(xc * xc, axis=-1, keepdims=True)
        y = xc * lax.rsqrt(var + LN_EPS) * g_ref[...] + b_ref[...]
        for o_ref in out_refs:
            o_ref[rows, :] = y.astype(o_ref.dtype)
        return carry

    lax.fori_loop(0, x_ref.shape[0] // rs, sub_block, 0)


def _ln_kernel(x_ref, g_ref, b_ref, of_ref, ob_ref, *, rs):
    _ln_rows(x_ref, g_ref, b_ref, (of_ref, ob_ref), rs)


def _ln_split_kernel(x_ref, g_ref, b_ref, head_ref, tail_ref, *, rs, n_head):
    i = pl.program_id(0)

    @pl.when(i < n_head)
    def _head():
        _ln_rows(x_ref, g_ref, b_ref, (head_ref,), rs)

    @pl.when(i >= n_head)
    def _tail():
        _ln_rows(x_ref, g_ref, b_ref, (tail_ref,), rs)


def _ln_split_call(x, g, b, tr, m_head):
    m, d = x.shape
    n_head = m_head // tr
    return pl.pallas_call(
        functools.partial(_ln_split_kernel, rs=_pick_tile(tr, (32, 16)), n_head=n_head),
        grid=(m // tr,),
        in_specs=[pl.BlockSpec((tr, d), lambda i: (i, 0)),
                  pl.BlockSpec((1, d), lambda i: (0, 0)),
                  pl.BlockSpec((1, d), lambda i: (0, 0))],
        out_specs=[pl.BlockSpec((tr, d), lambda i: (jnp.minimum(i, n_head - 1), 0)),
                   pl.BlockSpec((tr, d), lambda i: (jnp.maximum(i - n_head, 0), 0))],
        out_shape=[jax.ShapeDtypeStruct((m_head, d), jnp.float32),
                   jax.ShapeDtypeStruct((m - m_head, d), jnp.float32)],
        compiler_params=_params(("arbitrary",)),
        name="layer_norm_out",
    )(x, g, b)


def _ln_call(x, g, b, tr):
    m, d = x.shape
    return pl.pallas_call(
        functools.partial(_ln_kernel, rs=_pick_tile(tr, (32, 16))),
        grid=(m // tr,),
        in_specs=[pl.BlockSpec((tr, d), lambda i: (i, 0)),
                  pl.BlockSpec((1, d), lambda i: (0, 0)),
                  pl.BlockSpec((1, d), lambda i: (0, 0))],
        out_specs=[pl.BlockSpec((tr, d), lambda i: (i, 0)),
                   pl.BlockSpec((tr, d), lambda i: (i, 0))],
        out_shape=[jax.ShapeDtypeStruct((m, d), jnp.float32),
                   jax.ShapeDtypeStruct((m, d), jnp.bfloat16)],
        compiler_params=_params(("parallel",)),
        name="layer_norm",
    )(x, g, b)


def kernel(x_prompt, x_sample, cache_k, cache_v, cache_kidx, state_conv, state_h, p_prompt, p_sample, w_in, conv_w, conv_b, gate_a_w, gate_a_b, gate_x_w, gate_x_b, lru_lambda, w_out, ln_g, ln_b, ple_proj, ple_gate):
    bf16 = jnp.bfloat16
    n_bp, t_p, d = x_prompt.shape
    n_bs, t_s, _ = x_sample.shape
    depth = w_in.shape[0]
    n_past = cache_k.shape[2]
    lru_w = conv_w.shape[2]
    att_w = d - lru_w
    n_heads = att_w // HEAD_DIM
    kv_w = N_KV_HEADS * HEAD_DIM
    qi_w = N_IDX_HEADS * IDX_DIM
    alpha = (2 * depth) ** 0.25
    m_p, m_s = n_bp * t_p, n_bs * t_s
    m = m_p + m_s
    assert t_p % KEY_BLOCK == 0 and m_p % t_s == 0 and t_s % SUBLANES == 0 and n_past % KEY_BLOCK == 0
    assert lru_w // N_LRU_BLOCKS == LANES and CHUNK == 1 << CHUNK_SHIFT

    o_u, o_gl = 0, lru_w
    o_q = 2 * lru_w
    o_k = o_q + att_w
    o_v = o_k + kv_w
    o_ga = o_v + kv_w
    o_qi = o_ga + att_w
    o_ki = o_qi + qi_w
    c_qi, c_q = 0, qi_w
    c_ga = c_q + att_w
    c_u = c_ga + att_w
    c_gl = c_u + lru_w
    c_k = c_gl + lru_w
    c_v = c_k + kv_w
    n_main = c_v + kv_w

    tm = _pick_tile(m, (768, 512, 384, 256, 128))
    tn_in = _pick_tile(n_main, (1024, 512))
    tn_out = _pick_tile(d, (512,))
    tr_ln = _pick_tile(m, (256, 128))
    tr_out = [c for c in (256, 128, 64, 32, 16, 8) if m_p % c == 0 and m_s % c == 0][0]
    ct_p = _pick_tile(lru_w, (256,))
    tq = _pick_tile(t_p, (256,))

    xf = jnp.concatenate([x_prompt.reshape(m_p, d), x_sample.reshape(m_s, d)], axis=0)
    xb = xf.astype(bf16)
    hist_pad = ((0, 0), (SUBLANES - (CONV_W - 1), 0), (0, 0))
    zero_conv = jnp.zeros((n_bp, SUBLANES, lru_w), jnp.float32)
    zero_h = jnp.zeros((n_bp, 1, lru_w), jnp.float32)
    past = (cache_k.reshape(depth, n_bs, n_past * N_KV_HEADS, HEAD_DIM),
            cache_v.reshape(depth, n_bs, n_past * N_KV_HEADS, HEAD_DIM), cache_kidx)

    yl = jnp.zeros((m, lru_w), bf16)
    ya = jnp.zeros((m, att_w), bf16)
    cast_rows = _pick_tile(d, (512, 256, 128))
    w_out16 = _cast_call(w_out, cast_rows)
    ple_gate16 = _cast_call(ple_gate, cast_rows)
    outs_p, outs_s = [], []
    for i in range(depth):
        wl = w_in[i]
        seg = lambda o, n: wl[:, o:o + n]
        w_main = jnp.concatenate(
            [seg(o_qi, qi_w), seg(o_q, att_w), seg(o_ga, att_w), seg(o_u, lru_w), seg(o_gl, lru_w),
             seg(o_k, kv_w), seg(o_v, kv_w)], axis=1).astype(bf16)
        w_side = wl[:, o_ki:].astype(bf16)

        z, k_rows, v_rows = _in_proj_call(xb, w_main, tm, tn_in, c_k)
        ki, wi = _idx_proj_call(xb, w_side, tm)
        wt_p = wi[:m_p].T
        wrow_s = wi[m_p:].reshape(n_bs, t_s, N_IDX_HEADS).transpose(0, 2, 1).reshape(n_bs, 1, -1)

        lru_args = (conv_w[i], conv_b[i][None], gate_a_w[i].astype(bf16), gate_a_b[i][None],
                    gate_x_w[i].astype(bf16), gate_x_b[i][None], lru_lambda[i][None])
        yl, conv_p, h_p = _lru_call(z, 0, n_bp, t_p, ct_p, c_u, c_gl, zero_conv, zero_h, *lru_args,
                                    pos0=0, y_rows=m, y_prev=yl)
        yl, conv_s, h_s = _lru_call(z, m_p // t_s, n_bs, t_s, lru_w, c_u, c_gl,
                                    jnp.pad(state_conv[i], hist_pad), state_h[i][:, None, :], *lru_args,
                                    pos0=n_past, y_rows=m, y_prev=yl)
        conv_p, conv_s = conv_p[:, :CONV_W - 1], conv_s[:, :CONV_W - 1]

        cols = (c_qi, c_q, c_ga, c_k, c_v)
        ya = _attn_call(z, ki, wt_p, ya, n_b=n_bp, t_len=t_p, tq=tq, cols=cols)
        ya = _decode_attn_call(z, ki, wrow_s, past, ya, layer=i, row_blk0=m_p // t_s, n_b=n_bs, t_q=t_s,
                               pos0=n_past, cols=cols)

        p = jnp.concatenate([p_prompt[i].reshape(m_p, -1), p_sample[i].reshape(m_s, -1)], axis=0).astype(bf16)
        pre = _out_call(yl, ya, xb, p, xf, w_out16, ple_gate16, ple_proj[i].astype(bf16), i, alpha, tm,
                        tn_out)
        if i + 1 < depth:
            xf, xb = _ln_call(pre, ln_g[i][None], ln_b[i][None], tr_ln)
        else:
            y_p, y_s = _ln_split_call(pre, ln_g[i][None], ln_b[i][None], tr_out, m_p)

        hp = m_p * N_KV_HEADS
        outs_p.append((k_rows[:hp].reshape(n_bp, t_p, N_KV_HEADS, HEAD_DIM),
                       v_rows[:hp].reshape(n_bp, t_p, N_KV_HEADS, HEAD_DIM),
                       ki[:m_p].reshape(n_bp, t_p, IDX_DIM), conv_p, h_p[:, 0, :]))
        outs_s.append((k_rows[hp:].reshape(n_bs, t_s, N_KV_HEADS, HEAD_DIM),
                       v_rows[hp:].reshape(n_bs, t_s, N_KV_HEADS, HEAD_DIM),
                       ki[m_p:].reshape(n_bs, t_s, IDX_DIM), conv_s, h_s[:, 0, :]))

    stack = lambda outs, j: jnp.stack([o[j] for o in outs])
    return (y_p.reshape(n_bp, t_p, d), y_s.reshape(n_bs, t_s, d),
            stack(outs_p, 0), stack(outs_p, 1), stack(outs_p, 2), stack(outs_p, 3), stack(outs_p, 4),
            stack(outs_s, 0), stack(outs_s, 1), stack(outs_s, 2), stack(outs_s, 3), stack(outs_s, 4))
```

```python
import functools

import jax
import jax.numpy as jnp
from jax import lax
from jax.experimental import pallas as pl
from jax.experimental.pallas import tpu as pltpu

CHUNK = 64
CHUNK_SHIFT = 6
N_LRU_BLOCKS = 16
CONV_W = 4
LRU_C = 8.0
HEAD_DIM = 128
N_KV_HEADS = 4
N_IDX_HEADS = 32
IDX_DIM = 128
TOPK_MAX = 256
LN_EPS = 1e-5
LOG2_E = 1.4426950408889634

LANES = 128
SUBLANES = 8
VMEM_LIMIT_BYTES = 60 * 1024 * 1024

KEY_BLOCK = 256
KEY_BLOCK_SHIFT = 8

INT_MIN = -(2 ** 31)
KEY_NEG_INF = -2139095041

_NT = (((1,), (1,)), ((), ()))


def _pick_tile(n, candidates):
    for c in candidates:
        if n % c == 0:
            return c
    return n


def _params(sem):
    return pltpu.CompilerParams(dimension_semantics=sem, vmem_limit_bytes=VMEM_LIMIT_BYTES)


def _pack_rows_kernel(head_ref, tail_ref, of_ref, ob_ref, *, n_head):
    i = pl.program_id(0)

    @pl.when(i < n_head)
    def _head():
        of_ref[...] = head_ref[...]
        ob_ref[...] = head_ref[...].astype(ob_ref.dtype)

    @pl.when(i >= n_head)
    def _tail():
        of_ref[...] = tail_ref[...]
        ob_ref[...] = tail_ref[...].astype(ob_ref.dtype)


def _pack_rows_call(head, tail, tr):
    (m_h, d), m_t = head.shape, tail.shape[0]
    n_head = m_h // tr
    out_spec = pl.BlockSpec((tr, d), lambda i: (i, 0))
    return pl.pallas_call(
        functools.partial(_pack_rows_kernel, n_head=n_head),
        grid=((m_h + m_t) // tr,),
        in_specs=[pl.BlockSpec((tr, d), lambda i: (jnp.minimum(i, n_head - 1), 0)),
                  pl.BlockSpec((tr, d), lambda i: (jnp.maximum(i - n_head, 0), 0))],
        out_specs=[out_spec, out_spec],
        out_shape=[jax.ShapeDtypeStruct((m_h + m_t, d), jnp.float32),
                   jax.ShapeDtypeStruct((m_h + m_t, d), jnp.bfloat16)],
        compiler_params=_params(("parallel",)),
        name="pack_rows",
    )(head, tail)


def _cast_kernel(w_ref, o_ref, *, rs):
    def piece(c, carry):
        rows = pl.ds(pl.multiple_of(c * rs, rs), rs)
        o_ref[rows, :] = w_ref[rows, :].astype(o_ref.dtype)
        return carry

    lax.fori_loop(0, w_ref.shape[0] // rs, piece, 0)


def _cast_call(w, rows):
    depth, k, n = w.shape
    spec = pl.BlockSpec((None, rows, n), lambda l, i: (l, i, 0))
    return pl.pallas_call(
        functools.partial(_cast_kernel, rs=_pick_tile(rows, (32, 16))),
        grid=(depth, k // rows),
        in_specs=[spec],
        out_specs=spec,
        out_shape=jax.ShapeDtypeStruct(w.shape, jnp.bfloat16),
        compiler_params=_params(("parallel", "parallel")),
        name="cast_bf16",
    )(w)


def _in_proj_kernel(x_ref, w_ref, z_ref, k_ref, v_ref, *, j_kv):
    z_ref[...] = jnp.dot(x_ref[...], w_ref[...], preferred_element_type=jnp.float32)

    @pl.when(pl.program_id(1) == j_kv)
    def _emit_kv():
        tm = z_ref.shape[0]
        kv_w = N_KV_HEADS * HEAD_DIM
        for g in range(N_KV_HEADS):
            rows = pl.ds(g, tm, stride=N_KV_HEADS)
            k_ref[rows, :] = z_ref[:, g * HEAD_DIM:(g + 1) * HEAD_DIM]
            v_ref[rows, :] = z_ref[:, kv_w + g * HEAD_DIM:kv_w + (g + 1) * HEAD_DIM]


def _in_proj_call(x, w, tm, tn, k_col):
    m, kdim = x.shape
    n = w.shape[1]
    assert k_col % tn == 0 and tn == 2 * N_KV_HEADS * HEAD_DIM
    kv_shape = jax.ShapeDtypeStruct((m * N_KV_HEADS, HEAD_DIM), jnp.float32)
    kv_spec = pl.BlockSpec((tm * N_KV_HEADS, HEAD_DIM), lambda i, j: (i, 0))
    return pl.pallas_call(
        functools.partial(_in_proj_kernel, j_kv=k_col // tn),
        grid=(m // tm, n // tn),
        in_specs=[pl.BlockSpec((tm, kdim), lambda i, j: (i, 0)),
                  pl.BlockSpec((kdim, tn), lambda i, j: (0, j))],
        out_specs=[pl.BlockSpec((tm, tn), lambda i, j: (i, j)), kv_spec, kv_spec],
        out_shape=[jax.ShapeDtypeStruct((m, n), jnp.float32), kv_shape, kv_shape],
        compiler_params=_params(("parallel", "arbitrary")),
        name="in_proj",
    )(x, w)


def _idx_proj_kernel(x_ref, w_ref, ki_ref, wi_ref):
    r = jnp.dot(x_ref[...], w_ref[...], preferred_element_type=jnp.float32)
    ki_ref[...] = r[:, 0:IDX_DIM]
    wi_ref[...] = r[:, IDX_DIM:]


def _idx_proj_call(x, w, tm):
    m, kdim = x.shape
    n = w.shape[1]
    return pl.pallas_call(
        _idx_proj_kernel,
        grid=(m // tm,),
        in_specs=[pl.BlockSpec((tm, kdim), lambda i: (i, 0)),
                  pl.BlockSpec((kdim, n), lambda i: (0, 0))],
        out_specs=[pl.BlockSpec((tm, IDX_DIM), lambda i: (i, 0)),
                   pl.BlockSpec((tm, n - IDX_DIM), lambda i: (i, 0))],
        out_shape=[jax.ShapeDtypeStruct((m, IDX_DIM), jnp.float32),
                   jax.ShapeDtypeStruct((m, n - IDX_DIM), jnp.float32)],
        compiler_params=_params(("parallel",)),
        name="idx_proj",
    )(x, w)


def _sigmoid(x):
    return 0.5 * jnp.tanh(0.5 * x) + 0.5


def _lru_kernel(u_ref, g_ref, cbuf_ref, h0_ref, cw_ref, cb_ref, gaw_ref, gab_ref,
                gxw_ref, gxb_ref, lam_ref, y_ref, cnew_ref, hlast_ref,
                full_ref, *, t_len, ct, rc, pos0):
    full_ref[0:SUBLANES, :] = cbuf_ref[...]
    full_ref[SUBLANES:SUBLANES + t_len, :] = u_ref[...]
    cnew_ref[...] = pltpu.roll(full_ref[t_len:t_len + SUBLANES, :], CONV_W - 1, 0)

    n_tiles = rc // SUBLANES
    row = lax.broadcasted_iota(jnp.int32, (SUBLANES, ct), 0)
    taps = [jnp.broadcast_to(cw_ref[CONV_W - 1 - s:CONV_W - s, :], (SUBLANES, ct)) for s in range(CONV_W)]
    cb = jnp.broadcast_to(cb_ref[...], (SUBLANES, ct))
    nlam = -lam_ref[...]
    softplus = jnp.maximum(nlam, 0.0) + jnp.log1p(jnp.exp(-jnp.abs(nlam)))
    log_a_scale = -LRU_C * softplus
    gab = gab_ref[...]
    gxb = gxb_ref[...]

    def chunk(c, h):
        r0 = pl.multiple_of(c * rc, rc)
        conv_tiles = []
        for t in range(n_tiles):
            prev = full_ref[pl.ds(r0 + t * SUBLANES, SUBLANES), :]
            cur = full_ref[pl.ds(r0 + (t + 1) * SUBLANES, SUBLANES), :]
            acc = cb + cur * taps[0]
            for s in range(1, CONV_W):
                shifted = pltpu.roll(jnp.where(row >= SUBLANES - s, prev, cur), s, 0)
                acc = acc + shifted * taps[s]
            conv_tiles.append(acc)
        conv = jnp.concatenate(conv_tiles, axis=0) if n_tiles > 1 else conv_tiles[0]
        conv16 = conv.astype(jnp.bfloat16)
        r_parts, i_parts = [], []
        for n in range(ct // LANES):
            ub = conv16[:, n * LANES:(n + 1) * LANES]
            r_parts.append(jnp.dot(ub, gaw_ref[n], preferred_element_type=jnp.float32))
            i_parts.append(jnp.dot(ub, gxw_ref[n], preferred_element_type=jnp.float32))
        r = _sigmoid(jnp.concatenate(r_parts, axis=1) + gab)
        gi = _sigmoid(jnp.concatenate(i_parts, axis=1) + gxb)
        log_a = r * log_a_scale
        th = jnp.tanh(log_a)
        mult = jnp.sqrt(-2.0 * th / (1.0 - th))
        a = jnp.exp(log_a)
        b = conv * gi * mult

        scanned = []
        for t in range(n_tiles):
            ac = a[t * SUBLANES:(t + 1) * SUBLANES, :]
            bc = b[t * SUBLANES:(t + 1) * SUBLANES, :]
            if t == 0 and pos0 == 0:
                start = (row == 0) & (r0 == 0)
                bc = jnp.where(start, conv[0:SUBLANES, :] * gi[0:SUBLANES, :], bc)
            for s in (1, 2, 4):
                a_sh = jnp.where(row >= s, pltpu.roll(ac, s, 0), 1.0)
                b_sh = jnp.where(row >= s, pltpu.roll(bc, s, 0), 0.0)
                bc = ac * b_sh + bc
                ac = ac * a_sh
            scanned.append((ac, bc))
        h_rows = []
        for ac, bc in scanned:
            h_rows.append(ac * h + bc)
            h = ac[SUBLANES - 1:SUBLANES, :] * h + bc[SUBLANES - 1:SUBLANES, :]
        h_seq = jnp.concatenate(h_rows, axis=0) if n_tiles > 1 else h_rows[0]
        g = g_ref[pl.ds(r0, rc), :]
        y_ref[pl.ds(r0, rc), :] = (h_seq * (g * _sigmoid(g))).astype(y_ref.dtype)
        return h

    hlast_ref[...] = lax.fori_loop(0, t_len // rc, chunk, h0_ref[...])


def _drop_ref(fn, pos):
    def body(*refs):
        return fn(*refs[:pos], *refs[pos + 1:])
    return body


def _lru_call(z, row_blk0, n_b, t_len, ct, u_col, g_col, cbuf, h0, cw, cb, gaw, gab, gxw, gxb, lam, pos0,
              y_rows, y_prev=None):
    lru_w = cw.shape[1]
    n_c = lru_w // ct
    nb_blk = ct // LANES
    rc = _pick_tile(t_len, (128, 64, 32, 16))
    kern = functools.partial(_lru_kernel, t_len=t_len, ct=ct, rc=rc, pos0=pos0)
    vec = lambda: pl.BlockSpec((1, ct), lambda b, c: (0, c))
    args = [z, z, cbuf, h0, cw, cb, gaw, gab, gxw, gxb, lam]
    extra_specs, aliases = [], {}
    if y_prev is not None:
        kern = _drop_ref(kern, len(args))
        extra_specs = [pl.BlockSpec(memory_space=pl.ANY)]
        aliases = {len(args): 0}
        args.append(y_prev)
    return pl.pallas_call(
        kern,
        grid=(n_b, n_c),
        input_output_aliases=aliases,
        in_specs=[
            pl.BlockSpec((t_len, ct), lambda b, c: (row_blk0 + b, u_col // ct + c)),
            pl.BlockSpec((t_len, ct), lambda b, c: (row_blk0 + b, g_col // ct + c)),
            pl.BlockSpec((None, SUBLANES, ct), lambda b, c: (b, 0, c)),
            pl.BlockSpec((None, 1, ct), lambda b, c: (b, 0, c)),
            pl.BlockSpec((CONV_W, ct), lambda b, c: (0, c)),
            vec(),
            pl.BlockSpec((nb_blk, LANES, LANES), lambda b, c: (c, 0, 0)),
            vec(),
            pl.BlockSpec((nb_blk, LANES, LANES), lambda b, c: (c, 0, 0)),
            vec(),
            vec(),
        ] + extra_specs,
        out_specs=[
            pl.BlockSpec((t_len, ct), lambda b, c: (row_blk0 + b, c)),
            pl.BlockSpec((None, SUBLANES, ct), lambda b, c: (b, 0, c)),
            pl.BlockSpec((None, 1, ct), lambda b, c: (b, 0, c)),
        ],
        out_shape=[
            jax.ShapeDtypeStruct((y_rows, lru_w), jnp.bfloat16),
            jax.ShapeDtypeStruct((n_b, SUBLANES, lru_w), jnp.float32),
            jax.ShapeDtypeStruct((n_b, 1, lru_w), jnp.float32),
        ],
        scratch_shapes=[
            pltpu.VMEM((t_len + SUBLANES, ct), jnp.float32),
        ],
        compiler_params=_params(("parallel", "parallel")),
        name="rg_lru",
    )(*args)


def _tree(x, op):
    tiles = [x[i:i + SUBLANES] for i in range(0, x.shape[0], SUBLANES)]
    while len(tiles) > 1:
        pairs = [op(tiles[i], tiles[i + 1]) for i in range(0, len(tiles) - 1, 2)]
        tiles = pairs + tiles[len(pairs) * 2:]
    return tiles[0]


def _key_to_float(key):
    return pltpu.bitcast(jnp.where(key < 0, key ^ 0x7FFFFFFF, key), jnp.float32)


def _kth_largest(count_ge, k, lanes):
    thr = jnp.where(count_ge(jnp.zeros((1, lanes), jnp.float32)) >= k, 0, INT_MIN).astype(jnp.int32)

    def bit_body(i, t):
        cand = t + lax.shift_left(jnp.int32(1), 30 - i)
        return jnp.where(count_ge(_key_to_float(cand)) >= k, cand, t)

    thr = lax.fori_loop(0, 31, bit_body, thr)
    return _key_to_float(jnp.maximum(thr, KEY_NEG_INF + 1))


def _n_key_chunks(last_q, s_all):
    n_allowed = jnp.minimum(
        lax.shift_left(lax.shift_right_logical(last_q, CHUNK_SHIFT) + 1, CHUNK_SHIFT), s_all)
    return lax.shift_right_logical(n_allowed + (KEY_BLOCK - 1), KEY_BLOCK_SHIFT)


def _attn_kernel(qi_ref, q_ref, ga_ref, wt_ref, k_ref, v_ref, ki_ref, y_ref,
                 kb_ref, vt_ref, kib_ref, qis_ref, qs_ref, sc_ref, m_ref, s_ref, ot_ref,
                 *, tq, n_sel, n_heads):
    kb = KEY_BLOCK
    group = n_heads // N_KV_HEADS
    s_all = sc_ref.shape[0]
    qb = pl.program_id(1)

    @pl.when(qb == 0)
    def _assemble_keys():
        for blk in range(s_all // LANES):
            rows = slice(blk * LANES, (blk + 1) * LANES)
            kib_ref[rows, :] = ki_ref[rows, :].astype(jnp.bfloat16)
            for g in range(N_KV_HEADS):
                cols = slice(g * HEAD_DIM, (g + 1) * HEAD_DIM)
                kb_ref[g, rows, :] = k_ref[rows, cols].astype(jnp.bfloat16)
                lane0 = (blk * LANES) % kb
                vt_ref[g, (blk * LANES) // kb, :, lane0:lane0 + LANES] = (
                    v_ref[rows, cols].T.astype(jnp.bfloat16))

    for h in range(N_IDX_HEADS):
        qis_ref[h * tq:(h + 1) * tq, :] = qi_ref[:, h * IDX_DIM:(h + 1) * IDX_DIM].astype(jnp.bfloat16)
    q_scale = LOG2_E * HEAD_DIM ** -0.5
    for h in range(n_heads):
        qs_ref[h * tq:(h + 1) * tq, :] = (
            q_ref[:, h * HEAD_DIM:(h + 1) * HEAD_DIM] * q_scale).astype(jnp.bfloat16)

    w = wt_ref[...] * (N_IDX_HEADS ** -0.5)
    qpos = qb * tq + lax.broadcasted_iota(jnp.int32, (1, tq), 1)
    qchunk = lax.shift_right_logical(qpos, CHUNK_SHIFT)
    n_chunks = _n_key_chunks(qb * tq + (tq - 1), s_all)

    def chunk_rows(c):
        return pl.ds(pl.multiple_of(c * kb, kb), kb)

    def score_chunk(c, carry):
        rows = chunk_rows(c)
        kic = kib_ref[rows, :]
        sc = jnp.zeros((kb, tq), jnp.float32)
        for hg in range(N_IDX_HEADS // 4):
            rel = lax.dot_general(kic, qis_ref[hg * 4 * tq:(hg + 1) * 4 * tq, :], _NT,
                                  preferred_element_type=jnp.float32)
            for j in range(4):
                h = hg * 4 + j
                sc = sc + jnp.maximum(rel[:, j * tq:(j + 1) * tq], 0.0) * w[h:h + 1, :]
        sc = sc * (IDX_DIM ** -0.5)
        kpos = c * kb + lax.broadcasted_iota(jnp.int32, (kb, tq), 0)
        allowed = lax.shift_right_logical(kpos, CHUNK_SHIFT) <= qchunk
        sc_ref[rows, :] = jnp.where(allowed, sc, -jnp.inf)
        return carry

    lax.fori_loop(0, n_chunks, score_chunk, 0)

    def count_ge(cand_f):
        def body(c, acc):
            return acc + _tree(jnp.where(sc_ref[chunk_rows(c), :] >= cand_f, 1.0, 0.0), jnp.add)
        acc = lax.fori_loop(0, n_chunks, body, jnp.zeros((SUBLANES, tq), jnp.float32))
        return jnp.sum(acc, axis=0, keepdims=True)

    thr_f = _kth_largest(count_ge, n_sel, tq)

    m_ref[...] = jnp.full(m_ref.shape, -jnp.inf, jnp.float32)
    s_ref[...] = jnp.zeros(s_ref.shape, jnp.float32)
    ot_ref[...] = jnp.zeros(ot_ref.shape, jnp.float32)
    slopes2 = [LOG2_E * 2.0 ** (-8.0 * (h + 1) / n_heads) for h in range(n_heads)]

    def attn_chunk(c, carry):
        rows = chunk_rows(c)
        kpos = c * kb + lax.broadcasted_iota(jnp.int32, (kb, tq), 0)
        dist = jnp.abs(qpos - kpos).astype(jnp.float32)
        dmc = jnp.where(sc_ref[rows, :] >= thr_f, dist, jnp.inf)
        for g in range(N_KV_HEADS):
            qg = qs_ref[g * group * tq:(g + 1) * group * tq, :]
            lg = lax.dot_general(kb_ref[g, rows, :], qg, _NT, preferred_element_type=jnp.float32)
            ps, alphas = [], []
            for j in range(group):
                h = g * group + j
                lj = lg[:, j * tq:(j + 1) * tq] - slopes2[h] * dmc
                m_old = m_ref[h:h + 1, :]
                m_new = jnp.maximum(m_old, jnp.max(_tree(lj, jnp.maximum), axis=0, keepdims=True))
                m_safe = jnp.where(m_new == -jnp.inf, 0.0, m_new)
                alpha = jnp.exp2(m_old - m_safe)
                p = jnp.exp2(lj - m_safe)
                srows = slice(h * SUBLANES, (h + 1) * SUBLANES)
                s_ref[srows, :] = s_ref[srows, :] * alpha + _tree(p, jnp.add)
                m_ref[h:h + 1, :] = m_new
                ps.append(p.astype(jnp.bfloat16))
                alphas.append(alpha)
            pv = jnp.dot(vt_ref[g, c], jnp.concatenate(ps, axis=1), preferred_element_type=jnp.float32)
            ot_ref[g] = ot_ref[g] * jnp.concatenate(alphas, axis=1) + pv
        return carry

    lax.fori_loop(0, n_chunks, attn_chunk, 0)

    for h in range(n_heads):
        g, j = divmod(h, group)
        s = jnp.sum(s_ref[h * SUBLANES:(h + 1) * SUBLANES, :], axis=0, keepdims=True)
        o = (ot_ref[g, :, j * tq:(j + 1) * tq] / s).T
        gate = ga_ref[:, h * HEAD_DIM:(h + 1) * HEAD_DIM]
        y_ref[:, h * HEAD_DIM:(h + 1) * HEAD_DIM] = (o * (gate * _sigmoid(gate))).astype(y_ref.dtype)


def _attn_call(z, ki, wt, y_prev, *, n_b, t_len, tq, cols):
    qi_col, q_col, ga_col, k_col, v_col = cols
    n_heads = (ga_col - q_col) // HEAD_DIM
    att_w = n_heads * HEAD_DIM
    qi_w = N_IDX_HEADS * IDX_DIM
    kv_w = N_KV_HEADS * HEAD_DIM
    n_q = t_len // tq
    n_sel = min(TOPK_MAX, t_len // 4)
    kern = functools.partial(_attn_kernel, tq=tq, n_sel=n_sel, n_heads=n_heads)
    qrow = lambda b, q: b * n_q + q
    n_args = 7
    return pl.pallas_call(
        _drop_ref(kern, n_args),
        grid=(n_b, n_q),
        input_output_aliases={n_args: 0},
        in_specs=[
            pl.BlockSpec((tq, qi_w), lambda b, q: (qrow(b, q), qi_col // qi_w)),
            pl.BlockSpec((tq, att_w), lambda b, q: (qrow(b, q), q_col // att_w)),
            pl.BlockSpec((tq, att_w), lambda b, q: (qrow(b, q), ga_col // att_w)),
            pl.BlockSpec((N_IDX_HEADS, tq), lambda b, q: (0, qrow(b, q))),
            pl.BlockSpec((t_len, kv_w), lambda b, q: (b, k_col // kv_w)),
            pl.BlockSpec((t_len, kv_w), lambda b, q: (b, v_col // kv_w)),
            pl.BlockSpec((t_len, IDX_DIM), lambda b, q: (b, 0)),
            pl.BlockSpec(memory_space=pl.ANY),
        ],
        out_specs=pl.BlockSpec((tq, att_w), lambda b, q: (qrow(b, q), 0)),
        out_shape=jax.ShapeDtypeStruct(y_prev.shape, y_prev.dtype),
        scratch_shapes=[
            pltpu.VMEM((N_KV_HEADS, t_len, HEAD_DIM), jnp.bfloat16),
            pltpu.VMEM((N_KV_HEADS, t_len // KEY_BLOCK, HEAD_DIM, KEY_BLOCK), jnp.bfloat16),
            pltpu.VMEM((t_len, IDX_DIM), jnp.bfloat16),
            pltpu.VMEM((N_IDX_HEADS * tq, IDX_DIM), jnp.bfloat16),
            pltpu.VMEM((n_heads * tq, HEAD_DIM), jnp.bfloat16),
            pltpu.VMEM((t_len, tq), jnp.float32),
            pltpu.VMEM((n_heads, tq), jnp.float32),
            pltpu.VMEM((n_heads * SUBLANES, tq), jnp.float32),
            pltpu.VMEM((N_KV_HEADS, HEAD_DIM, (n_heads // N_KV_HEADS) * tq), jnp.float32),
        ],
        compiler_params=_params(("arbitrary", "arbitrary")),
        name="sparse_attn",
    )(z, z, z, wt, z, z, ki, y_prev)


def _decode_attn_kernel(qi_ref, q_ref, ga_ref, wrow_ref, kcur_ref, vcur_ref, kicur_ref,
                        kp_ref, vp_ref, kip_ref, y_ref,
                        kb_ref, vt_ref, kib_ref, qit_ref, bdt_ref, sc_ref, stage_ref,
                        *, t_q, n_past, pos0, n_sel, n_heads):
    kb = KEY_BLOCK
    group = n_heads // N_KV_HEADS
    hpt = LANES // t_q
    n_tiles = n_heads // hpt
    gpt = hpt // group
    s_all = n_past + t_q
    s_pad = sc_ref.shape[0]
    kv_w = N_KV_HEADS * HEAD_DIM
    tile_w = gpt * HEAD_DIM
    slot_shifts = [LANES >> (i + 1) for i in range(hpt.bit_length() - 1)]

    def sum_over_slots(x):
        for sh in slot_shifts:
            x = x + pltpu.roll(x, sh, 1)
        return x

    def load_block(k_src, v_src, ki_src, src_row, dst_row, heads_on_rows):
        rows = slice(src_row, src_row + LANES)
        dst = slice(dst_row, dst_row + LANES)
        kib_ref[dst, :] = ki_src[rows, 0:IDX_DIM].astype(jnp.bfloat16)
        for g in range(N_KV_HEADS):
            if heads_on_rows:
                idx = (pl.ds(src_row * N_KV_HEADS + g, LANES, stride=N_KV_HEADS), slice(None))
            else:
                idx = (rows, slice(g * HEAD_DIM, (g + 1) * HEAD_DIM))
            kb_ref[dst, g * HEAD_DIM:(g + 1) * HEAD_DIM] = k_src[idx].astype(jnp.bfloat16)
            lane0 = dst_row % kb
            vt_ref[g // gpt, dst_row // kb, (g % gpt) * HEAD_DIM:(g % gpt + 1) * HEAD_DIM,
                   lane0:lane0 + LANES] = v_src[idx].T.astype(jnp.bfloat16)

    for blk in range(n_past // LANES):
        load_block(kp_ref, vp_ref, kip_ref, blk * LANES, blk * LANES, True)
    stage_ref[...] = jnp.zeros(stage_ref.shape, jnp.float32)
    stage_ref[0, 0:t_q, 0:kv_w] = kcur_ref[...]
    stage_ref[1, 0:t_q, 0:kv_w] = vcur_ref[...]
    stage_ref[2, 0:t_q, 0:IDX_DIM] = kicur_ref[...]
    for blk in range((s_pad - n_past) // LANES):
        load_block(stage_ref.at[0], stage_ref.at[1], stage_ref.at[2], blk * LANES, n_past + blk * LANES,
                   False)

    for h in range(N_IDX_HEADS):
        qit_ref[h * t_q:(h + 1) * t_q, :] = qi_ref[:, h * IDX_DIM:(h + 1) * IDX_DIM].astype(jnp.bfloat16)
    bdt_ref[...] = jnp.zeros(bdt_ref.shape, bdt_ref.dtype)
    q_scale = LOG2_E * HEAD_DIM ** -0.5
    for h in range(n_heads):
        tile, slot = divmod(h, hpt)
        gl = slot // group
        bdt_ref[tile, slot * t_q:(slot + 1) * t_q, gl * HEAD_DIM:(gl + 1) * HEAD_DIM] = (
            q_ref[:, h * HEAD_DIM:(h + 1) * HEAD_DIM] * q_scale).astype(jnp.bfloat16)

    lane = lax.broadcasted_iota(jnp.int32, (1, LANES), 1)
    slot_l = lax.shift_right_logical(lane, t_q.bit_length() - 1)
    qpos = pos0 + (lane & (t_q - 1))
    qchunk = lax.shift_right_logical(qpos, CHUNK_SHIFT)
    w = wrow_ref[...] * (N_IDX_HEADS ** -0.5)
    n_chunks = s_pad // kb

    for c in range(n_chunks):
        rows = slice(c * kb, (c + 1) * kb)
        rel = lax.dot_general(kib_ref[rows, :], qit_ref[...], _NT, preferred_element_type=jnp.float32)
        x = jnp.maximum(rel, 0.0) * w
        acc = x[:, 0:LANES]
        for i in range(1, x.shape[1] // LANES):
            acc = acc + x[:, i * LANES:(i + 1) * LANES]
        sc = sum_over_slots(acc) * (IDX_DIM ** -0.5)
        kpos = c * kb + lax.broadcasted_iota(jnp.int32, (kb, LANES), 0)
        allowed = (lax.shift_right_logical(kpos, CHUNK_SHIFT) <= qchunk) & (kpos < s_all)
        sc_ref[rows, :] = jnp.where(allowed, sc, -jnp.inf)

    def count_ge(cand_f):
        return jnp.sum(_tree(jnp.where(sc_ref[...] >= cand_f, 1.0, 0.0), jnp.add), axis=0, keepdims=True)

    thr_f = _kth_largest(count_ge, n_sel, LANES)

    slopes2 = []
    for p in range(n_tiles):
        head = (p * hpt + slot_l + 1).astype(jnp.float32)
        slopes2.append(LOG2_E * jnp.exp2(head * (-8.0 / n_heads)))
    m = [jnp.full((1, LANES), -jnp.inf, jnp.float32) for _ in range(n_tiles)]
    s8 = [jnp.zeros((SUBLANES, LANES), jnp.float32) for _ in range(n_tiles)]
    ot = [jnp.zeros((tile_w, LANES), jnp.float32) for _ in range(n_tiles)]
    for c in range(n_chunks):
        rows = slice(c * kb, (c + 1) * kb)
        kpos = c * kb + lax.broadcasted_iota(jnp.int32, (kb, LANES), 0)
        dist = jnp.abs(qpos - kpos).astype(jnp.float32)
        dmc = jnp.where(sc_ref[rows, :] >= thr_f, dist, jnp.inf)
        for p in range(n_tiles):
            lg = lax.dot_general(kb_ref[rows, p * tile_w:(p + 1) * tile_w], bdt_ref[p], _NT,
                                 preferred_element_type=jnp.float32)
            lj = lg - slopes2[p] * dmc
            m_new = jnp.maximum(m[p], jnp.max(_tree(lj, jnp.maximum), axis=0, keepdims=True))
            m_safe = jnp.where(m_new == -jnp.inf, 0.0, m_new)
            alpha = jnp.exp2(m[p] - m_safe)
            pr = jnp.exp2(lj - m_safe)
            s8[p] = s8[p] * alpha + _tree(pr, jnp.add)
            m[p] = m_new
            pv = jnp.dot(vt_ref[p, c], pr.astype(jnp.bfloat16), preferred_element_type=jnp.float32)
            ot[p] = ot[p] * alpha + pv

    for p in range(n_tiles):
        o = ot[p] / jnp.sum(s8[p], axis=0, keepdims=True)
        for gl in range(gpt):
            o_t = o[gl * HEAD_DIM:(gl + 1) * HEAD_DIM, :].T
            for j in range(group):
                slot = gl * group + j
                h = p * hpt + slot
                gate = ga_ref[:, h * HEAD_DIM:(h + 1) * HEAD_DIM]
                y_ref[:, h * HEAD_DIM:(h + 1) * HEAD_DIM] = (
                    o_t[slot * t_q:(slot + 1) * t_q, :] * (gate * _sigmoid(gate))).astype(y_ref.dtype)


def _decode_attn_call(z, ki, wrow, past, y_prev, *, layer, row_blk0, n_b, t_q, pos0, cols):
    qi_col, q_col, ga_col, k_col, v_col = cols
    n_heads = (ga_col - q_col) // HEAD_DIM
    att_w = n_heads * HEAD_DIM
    qi_w = N_IDX_HEADS * IDX_DIM
    kv_w = N_KV_HEADS * HEAD_DIM
    group = n_heads // N_KV_HEADS
    hpt = LANES // t_q
    assert t_q & (t_q - 1) == 0 and LANES % t_q == 0 and n_heads % hpt == 0 and hpt % group == 0
    assert (N_IDX_HEADS * t_q) % LANES == 0
    n_past = past[2].shape[2]
    s_all = n_past + t_q
    cur_pad = ((t_q + KEY_BLOCK - 1) // KEY_BLOCK) * KEY_BLOCK
    s_pad = n_past + cur_pad
    n_sel = min(TOPK_MAX, s_all // 4)
    n_tiles = n_heads // hpt
    gpt = hpt // group
    kern = functools.partial(_decode_attn_kernel, t_q=t_q, n_past=n_past, pos0=pos0, n_sel=n_sel,
                             n_heads=n_heads)
    row = lambda b: row_blk0 + b
    n_args = 10
    return pl.pallas_call(
        _drop_ref(kern, n_args),
        grid=(n_b,),
        input_output_aliases={n_args: 0},
        in_specs=[
            pl.BlockSpec((t_q, qi_w), lambda b: (row(b), qi_col // qi_w)),
            pl.BlockSpec((t_q, att_w), lambda b: (row(b), q_col // att_w)),
            pl.BlockSpec((t_q, att_w), lambda b: (row(b), ga_col // att_w)),
            pl.BlockSpec((None, 1, N_IDX_HEADS * t_q), lambda b: (b, 0, 0)),
            pl.BlockSpec((t_q, kv_w), lambda b: (row(b), k_col // kv_w)),
            pl.BlockSpec((t_q, kv_w), lambda b: (row(b), v_col // kv_w)),
            pl.BlockSpec((t_q, IDX_DIM), lambda b: (row(b), 0)),
            pl.BlockSpec((None, None, n_past * N_KV_HEADS, HEAD_DIM), lambda b: (layer, b, 0, 0)),
            pl.BlockSpec((None, None, n_past * N_KV_HEADS, HEAD_DIM), lambda b: (layer, b, 0, 0)),
            pl.BlockSpec((None, None, n_past, IDX_DIM), lambda b: (layer, b, 0, 0)),
            pl.BlockSpec(memory_space=pl.ANY),
        ],
        out_specs=pl.BlockSpec((t_q, att_w), lambda b: (row(b), 0)),
        out_shape=jax.ShapeDtypeStruct(y_prev.shape, y_prev.dtype),
        scratch_shapes=[
            pltpu.VMEM((s_pad, kv_w), jnp.bfloat16),
            pltpu.VMEM((N_KV_HEADS // gpt, s_pad // KEY_BLOCK, gpt * HEAD_DIM, KEY_BLOCK), jnp.bfloat16),
            pltpu.VMEM((s_pad, IDX_DIM), jnp.bfloat16),
            pltpu.VMEM((N_IDX_HEADS * t_q, IDX_DIM), jnp.bfloat16),
            pltpu.VMEM((n_tiles, LANES, gpt * HEAD_DIM), jnp.bfloat16),
            pltpu.VMEM((s_pad, LANES), jnp.float32),
            pltpu.VMEM((3, cur_pad, kv_w), jnp.float32),
        ],
        compiler_params=_params(("parallel",)),
        name="decode_attn",
    )(z, z, z, wrow, z, z, ki, *past, y_prev)


def _out_kernel(yl_ref, ya_ref, xb_ref, p_ref, xf_ref, wo_ref, wg_ref, wp_ref, o_ref, *, alpha):
    half = yl_ref.shape[1]
    mix = (jnp.dot(yl_ref[...], wo_ref[0:half, :], preferred_element_type=jnp.float32)
           + jnp.dot(ya_ref[...], wo_ref[half:, :], preferred_element_type=jnp.float32))
    gate = jax.nn.sigmoid(jnp.dot(xb_ref[...], wg_ref[...], preferred_element_type=jnp.float32))
    emb = jnp.dot(p_ref[...], wp_ref[...], preferred_element_type=jnp.float32)
    o_ref[...] = alpha * xf_ref[...] + mix + gate * emb


def _out_call(yl, ya, xb, p, xf, wo, wg, wp, layer, alpha, tm, tn):
    m, d = xf.shape
    half = yl.shape[1]
    ple = p.shape[1]
    row = lambda w: pl.BlockSpec((tm, w), lambda i, j: (i, 0))
    col = lambda k: pl.BlockSpec((k, tn), lambda i, j: (0, j))
    lcol = lambda k: pl.BlockSpec((None, k, tn), lambda i, j: (layer, 0, j))
    return pl.pallas_call(
        functools.partial(_out_kernel, alpha=alpha),
        grid=(m // tm, d // tn),
        in_specs=[row(half), row(ya.shape[1]), row(d), row(ple),
                  pl.BlockSpec((tm, tn), lambda i, j: (i, j)),
                  lcol(wo.shape[1]), lcol(d), col(ple)],
        out_specs=pl.BlockSpec((tm, tn), lambda i, j: (i, j)),
        out_shape=jax.ShapeDtypeStruct((m, d), jnp.float32),
        compiler_params=_params(("parallel", "parallel")),
        name="out_proj",
    )(yl, ya, xb, p, xf, wo, wg, wp)


def _ln_rows(x_ref, g_ref, b_ref, out_refs, rs):
    def sub_block(c, carry):
        rows = pl.ds(pl.multiple_of(c * rs, rs), rs)
        x = x_ref[rows, :]
        mu = jnp.mean(x, axis=-1, keepdims=True)
        xc = x - mu
        var = jnp.mean(xc * xc, axis=-1, keepdims=True)
        y = xc * lax.rsqrt(var + LN_EPS) * g_ref[...] + b_ref[...]
        for o_ref in out_refs:
            o_ref[rows, :] = y.astype(o_ref.dtype)
        return carry

    lax.fori_loop(0, x_ref.shape[0] // rs, sub_block, 0)


def _ln_kernel(x_ref, g_ref, b_ref, of_ref, ob_ref, *, rs):
    _ln_rows(x_ref, g_ref, b_ref, (of_ref, ob_ref), rs)


def _ln_split_kernel(x_ref, g_ref, b_ref, head_ref, tail_ref, *, rs, n_head):
    i = pl.program_id(0)

    @pl.when(i < n_head)
    def _head():
        _ln_rows(x_ref, g_ref, b_ref, (head_ref,), rs)

    @pl.when(i >= n_head)
    def _tail():
        _ln_rows(x_ref, g_ref, b_ref, (tail_ref,), rs)


def _ln_split_call(x, g, b, tr, m_head):
    m, d = x.shape
    n_head = m_head // tr
    return pl.pallas_call(
        functools.partial(_ln_split_kernel, rs=_pick_tile(tr, (32, 16)), n_head=n_head),
        grid=(m // tr,),
        in_specs=[pl.BlockSpec((tr, d), lambda i: (i, 0)),
                  pl.BlockSpec((1, d), lambda i: (0, 0)),
                  pl.BlockSpec((1, d), lambda i: (0, 0))],
        out_specs=[pl.BlockSpec((tr, d), lambda i: (jnp.minimum(i, n_head - 1), 0)),
                   pl.BlockSpec((tr, d), lambda i: (jnp.maximum(i - n_head, 0), 0))],
        out_shape=[jax.ShapeDtypeStruct((m_head, d), jnp.float32),
                   jax.ShapeDtypeStruct((m - m_head, d), jnp.float32)],
        compiler_params=_params(("arbitrary",)),
        name="layer_norm_out",
    )(x, g, b)


def _ln_call(x, g, b, tr):
    m, d = x.shape
    return pl.pallas_call(
        functools.partial(_ln_kernel, rs=_pick_tile(tr, (32, 16))),
        grid=(m // tr,),
        in_specs=[pl.BlockSpec((tr, d), lambda i: (i, 0)),
                  pl.BlockSpec((1, d), lambda i: (0, 0)),
                  pl.BlockSpec((1, d), lambda i: (0, 0))],
        out_specs=[pl.BlockSpec((tr, d), lambda i: (i, 0)),
                   pl.BlockSpec((tr, d), lambda i: (i, 0))],
        out_shape=[jax.ShapeDtypeStruct((m, d), jnp.float32),
                   jax.ShapeDtypeStruct((m, d), jnp.bfloat16)],
        compiler_params=_params(("parallel",)),
        name="layer_norm",
    )(x, g, b)


def kernel(x_prompt, x_sample, cache_k, cache_v, cache_kidx, state_conv, state_h, p_prompt, p_sample, w_in, conv_w, conv_b, gate_a_w, gate_a_b, gate_x_w, gate_x_b, lru_lambda, w_out, ln_g, ln_b, ple_proj, ple_gate):
    bf16 = jnp.bfloat16
    n_bp, t_p, d = x_prompt.shape
    n_bs, t_s, _ = x_sample.shape
    depth = w_in.shape[0]
    n_past = cache_k.shape[2]
    lru_w = conv_w.shape[2]
    att_w = d - lru_w
    n_heads = att_w // HEAD_DIM
    kv_w = N_KV_HEADS * HEAD_DIM
    qi_w = N_IDX_HEADS * IDX_DIM
    alpha = (2 * depth) ** 0.25
    m_p, m_s = n_bp * t_p, n_bs * t_s
    m = m_p + m_s
    assert t_p % KEY_BLOCK == 0 and m_p % t_s == 0 and t_s % SUBLANES == 0 and n_past % KEY_BLOCK == 0
    assert lru_w // N_LRU_BLOCKS == LANES and CHUNK == 1 << CHUNK_SHIFT

    o_u, o_gl = 0, lru_w
    o_q = 2 * lru_w
    o_k = o_q + att_w
    o_v = o_k + kv_w
    o_ga = o_v + kv_w
    o_qi = o_ga + att_w
    o_ki = o_qi + qi_w
    c_qi, c_q = 0, qi_w
    c_ga = c_q + att_w
    c_u = c_ga + att_w
    c_gl = c_u + lru_w
    c_k = c_gl + lru_w
    c_v = c_k + kv_w
    n_main = c_v + kv_w

    tm = _pick_tile(m, (768, 512, 384, 256, 128))
    tm_in = _pick_tile(m, (1056, 768, 512, 384, 256, 128))
    tn_in = _pick_tile(n_main, (1024, 512))
    tn_out = _pick_tile(d, (512,))
    tr_ln = _pick_tile(m, (256, 128))
    tr_out = [c for c in (256, 128, 64, 32, 16, 8) if m_p % c == 0 and m_s % c == 0][0]
    ct_p = _pick_tile(lru_w, (256,))
    tq = _pick_tile(t_p, (256,))

    xf, xb = _pack_rows_call(x_prompt.reshape(m_p, d), x_sample.reshape(m_s, d), tr_out)
    hist_pad = ((0, 0), (SUBLANES - (CONV_W - 1), 0), (0, 0))
    zero_conv = jnp.zeros((n_bp, SUBLANES, lru_w), jnp.float32)
    zero_h = jnp.zeros((n_bp, 1, lru_w), jnp.float32)
    past = (cache_k.reshape(depth, n_bs, n_past * N_KV_HEADS, HEAD_DIM),
            cache_v.reshape(depth, n_bs, n_past * N_KV_HEADS, HEAD_DIM), cache_kidx)

    yl = jnp.zeros((m, lru_w), bf16)
    ya = jnp.zeros((m, att_w), bf16)
    cast_rows = _pick_tile(d, (512, 256, 128))
    w_out16 = _cast_call(w_out, cast_rows)
    ple_gate16 = _cast_call(ple_gate, cast_rows)
    outs_p, outs_s = [], []
    for i in range(depth):
        wl = w_in[i]
        seg = lambda o, n: wl[:, o:o + n]
        w_main = jnp.concatenate(
            [seg(o_qi, qi_w), seg(o_q, att_w), seg(o_ga, att_w), seg(o_u, lru_w), seg(o_gl, lru_w),
             seg(o_k, kv_w), seg(o_v, kv_w)], axis=1).astype(bf16)
        w_side = wl[:, o_ki:].astype(bf16)

        z, k_rows, v_rows = _in_proj_call(xb, w_main, tm_in, tn_in, c_k)
        ki, wi = _idx_proj_call(xb, w_side, tm)
        wt_p = wi[:m_p].T
        wrow_s = wi[m_p:].reshape(n_bs, t_s, N_IDX_HEADS).transpose(0, 2, 1).reshape(n_bs, 1, -1)

        lru_args = (conv_w[i], conv_b[i][None], gate_a_w[i].astype(bf16), gate_a_b[i][None],
                    gate_x_w[i].astype(bf16), gate_x_b[i][None], lru_lambda[i][None])
        yl, conv_p, h_p = _lru_call(z, 0, n_bp, t_p, ct_p, c_u, c_gl, zero_conv, zero_h, *lru_args,
                                    pos0=0, y_rows=m, y_prev=yl)
        yl, conv_s, h_s = _lru_call(z, m_p // t_s, n_bs, t_s, lru_w, c_u, c_gl,
                                    jnp.pad(state_conv[i], hist_pad), state_h[i][:, None, :], *lru_args,
                                    pos0=n_past, y_rows=m, y_prev=yl)
        conv_p, conv_s = conv_p[:, :CONV_W - 1], conv_s[:, :CONV_W - 1]

        cols = (c_qi, c_q, c_ga, c_k, c_v)
        ya = _attn_call(z, ki, wt_p, ya, n_b=n_bp, t_len=t_p, tq=tq, cols=cols)
        ya = _decode_attn_call(z, ki, wrow_s, past, ya, layer=i, row_blk0=m_p // t_s, n_b=n_bs, t_q=t_s,
                               pos0=n_past, cols=cols)

        p = jnp.concatenate([p_prompt[i].reshape(m_p, -1), p_sample[i].reshape(m_s, -1)], axis=0).astype(bf16)
        pre = _out_call(yl, ya, xb, p, xf, w_out16, ple_gate16, ple_proj[i].astype(bf16), i, alpha, tm,
                        tn_out)
        if i + 1 < depth:
            xf, xb = _ln_call(pre, ln_g[i][None], ln_b[i][None], tr_ln)
        else:
            y_p, y_s = _ln_split_call(pre, ln_g[i][None], ln_b[i][None], tr_out, m_p)

        hp = m_p * N_KV_HEADS
        outs_p.append((k_rows[:hp].reshape(n_bp, t_p, N_KV_HEADS, HEAD_DIM),
                       v_rows[:hp].reshape(n_bp, t_p, N_KV_HEADS, HEAD_DIM),
                       ki[:m_p].reshape(n_bp, t_p, IDX_DIM), conv_p, h_p[:, 0, :]))
        outs_s.append((k_rows[hp:].reshape(n_bs, t_s, N_KV_HEADS, HEAD_DIM),
                       v_rows[hp:].reshape(n_bs, t_s, N_KV_HEADS, HEAD_DIM),
                       ki[m_p:].reshape(n_bs, t_s, IDX_DIM), conv_s, h_s[:, 0, :]))

    stack = lambda outs, j: jnp.stack([o[j] for o in outs])
    return (y_p.reshape(n_bp, t_p, d), y_s.reshape(n_bs, t_s, d),
            stack(outs_p, 0), stack(outs_p, 1), stack(outs_p, 2), stack(outs_p, 3), stack(outs_p, 4),
            stack(outs_s, 0), stack(outs_s, 1), stack(outs_s, 2), stack(outs_s, 3), stack(outs_s, 4))
```

```python
import functools

import jax
import jax.numpy as jnp
from jax import lax
from jax.experimental import pallas as pl
from jax.experimental.pallas import tpu as pltpu

CHUNK = 64
CHUNK_SHIFT = 6
N_LRU_BLOCKS = 16
CONV_W = 4
LRU_C = 8.0
HEAD_DIM = 128
N_KV_HEADS = 4
N_IDX_HEADS = 32
IDX_DIM = 128
TOPK_MAX = 256
LN_EPS = 1e-5
LOG2_E = 1.4426950408889634

LANES = 128
SUBLANES = 8
VMEM_LIMIT_BYTES = 60 * 1024 * 1024

KEY_BLOCK = 256
KEY_BLOCK_SHIFT = 8

INT_MIN = -(2 ** 31)
KEY_NEG_INF = -2139095041

_NT = (((1,), (1,)), ((), ()))


def _pick_tile(n, candidates):
    for c in candidates:
        if n % c == 0:
            return c
    return n


def _params(sem):
    return pltpu.CompilerParams(dimension_semantics=sem, vmem_limit_bytes=VMEM_LIMIT_BYTES)


def _pack_rows_kernel(head_ref, tail_ref, of_ref, ob_ref, *, n_head):
    i = pl.program_id(0)

    @pl.when(i < n_head)
    def _head():
        of_ref[...] = head_ref[...]
        ob_ref[...] = head_ref[...].astype(ob_ref.dtype)

    @pl.when(i >= n_head)
    def _tail():
        of_ref[...] = tail_ref[...]
        ob_ref[...] = tail_ref[...].astype(ob_ref.dtype)


def _pack_rows_call(head, tail, tr):
    (m_h, d), m_t = head.shape, tail.shape[0]
    n_head = m_h // tr
    out_spec = pl.BlockSpec((tr, d), lambda i: (i, 0))
    return pl.pallas_call(
        functools.partial(_pack_rows_kernel, n_head=n_head),
        grid=((m_h + m_t) // tr,),
        in_specs=[pl.BlockSpec((tr, d), lambda i: (jnp.minimum(i, n_head - 1), 0)),
                  pl.BlockSpec((tr, d), lambda i: (jnp.maximum(i - n_head, 0), 0))],
        out_specs=[out_spec, out_spec],
        out_shape=[jax.ShapeDtypeStruct((m_h + m_t, d), jnp.float32),
                   jax.ShapeDtypeStruct((m_h + m_t, d), jnp.bfloat16)],
        compiler_params=_params(("parallel",)),
        name="pack_rows",
    )(head, tail)


def _cast_kernel(w_ref, o_ref, *, rs):
    def piece(c, carry):
        rows = pl.ds(pl.multiple_of(c * rs, rs), rs)
        o_ref[rows, :] = w_ref[rows, :].astype(o_ref.dtype)
        return carry

    lax.fori_loop(0, w_ref.shape[0] // rs, piece, 0)


def _cast_call(w, rows):
    depth, k, n = w.shape
    spec = pl.BlockSpec((None, rows, n), lambda l, i: (l, i, 0))
    return pl.pallas_call(
        functools.partial(_cast_kernel, rs=_pick_tile(rows, (32, 16))),
        grid=(depth, k // rows),
        in_specs=[spec],
        out_specs=spec,
        out_shape=jax.ShapeDtypeStruct(w.shape, jnp.bfloat16),
        compiler_params=_params(("parallel", "parallel")),
        name="cast_bf16",
    )(w)


def _in_proj_kernel(x_ref, w_ref, z_ref, k_ref, v_ref, *, j_kv):
    z_ref[...] = jnp.dot(x_ref[...], w_ref[...], preferred_element_type=jnp.float32)

    @pl.when(pl.program_id(1) == j_kv)
    def _emit_kv():
        tm = z_ref.shape[0]
        kv_w = N_KV_HEADS * HEAD_DIM
        for g in range(N_KV_HEADS):
            rows = pl.ds(g, tm, stride=N_KV_HEADS)
            k_ref[rows, :] = z_ref[:, g * HEAD_DIM:(g + 1) * HEAD_DIM]
            v_ref[rows, :] = z_ref[:, kv_w + g * HEAD_DIM:kv_w + (g + 1) * HEAD_DIM]


def _in_proj_call(x, w, tm, tn, k_col):
    m, kdim = x.shape
    n = w.shape[1]
    assert k_col % tn == 0 and tn == 2 * N_KV_HEADS * HEAD_DIM
    kv_shape = jax.ShapeDtypeStruct((m * N_KV_HEADS, HEAD_DIM), jnp.float32)
    kv_spec = pl.BlockSpec((tm * N_KV_HEADS, HEAD_DIM), lambda i, j: (i, 0))
    return pl.pallas_call(
        functools.partial(_in_proj_kernel, j_kv=k_col // tn),
        grid=(m // tm, n // tn),
        in_specs=[pl.BlockSpec((tm, kdim), lambda i, j: (i, 0)),
                  pl.BlockSpec((kdim, tn), lambda i, j: (0, j))],
        out_specs=[pl.BlockSpec((tm, tn), lambda i, j: (i, j)), kv_spec, kv_spec],
        out_shape=[jax.ShapeDtypeStruct((m, n), jnp.float32), kv_shape, kv_shape],
        compiler_params=_params(("parallel", "arbitrary")),
        name="in_proj",
    )(x, w)


def _idx_proj_kernel(x_ref, w_ref, ki_ref, wi_ref):
    r = jnp.dot(x_ref[...], w_ref[...], preferred_element_type=jnp.float32)
    ki_ref[...] = r[:, 0:IDX_DIM]
    wi_ref[...] = r[:, IDX_DIM:]


def _idx_proj_call(x, w, tm):
    m, kdim = x.shape
    n = w.shape[1]
    return pl.pallas_call(
        _idx_proj_kernel,
        grid=(m // tm,),
        in_specs=[pl.BlockSpec((tm, kdim), lambda i: (i, 0)),
                  pl.BlockSpec((kdim, n), lambda i: (0, 0))],
        out_specs=[pl.BlockSpec((tm, IDX_DIM), lambda i: (i, 0)),
                   pl.BlockSpec((tm, n - IDX_DIM), lambda i: (i, 0))],
        out_shape=[jax.ShapeDtypeStruct((m, IDX_DIM), jnp.float32),
                   jax.ShapeDtypeStruct((m, n - IDX_DIM), jnp.float32)],
        compiler_params=_params(("parallel",)),
        name="idx_proj",
    )(x, w)


def _sigmoid(x):
    return 0.5 * jnp.tanh(0.5 * x) + 0.5


def _lru_kernel(u_ref, g_ref, cbuf_ref, h0_ref, cw_ref, cb_ref, gaw_ref, gab_ref,
                gxw_ref, gxb_ref, lam_ref, y_ref, cnew_ref, hlast_ref,
                full_ref, *, t_len, ct, rc, pos0):
    full_ref[0:SUBLANES, :] = cbuf_ref[...]
    full_ref[SUBLANES:SUBLANES + t_len, :] = u_ref[...]
    cnew_ref[...] = pltpu.roll(full_ref[t_len:t_len + SUBLANES, :], CONV_W - 1, 0)

    n_tiles = rc // SUBLANES
    row = lax.broadcasted_iota(jnp.int32, (SUBLANES, ct), 0)
    taps = [jnp.broadcast_to(cw_ref[CONV_W - 1 - s:CONV_W - s, :], (SUBLANES, ct)) for s in range(CONV_W)]
    cb = jnp.broadcast_to(cb_ref[...], (SUBLANES, ct))
    nlam = -lam_ref[...]
    softplus = jnp.maximum(nlam, 0.0) + jnp.log1p(jnp.exp(-jnp.abs(nlam)))
    log_a_scale = -LRU_C * softplus
    gab = gab_ref[...]
    gxb = gxb_ref[...]

    def chunk(c, h):
        r0 = pl.multiple_of(c * rc, rc)
        conv_tiles = []
        for t in range(n_tiles):
            prev = full_ref[pl.ds(r0 + t * SUBLANES, SUBLANES), :]
            cur = full_ref[pl.ds(r0 + (t + 1) * SUBLANES, SUBLANES), :]
            acc = cb + cur * taps[0]
            for s in range(1, CONV_W):
                shifted = pltpu.roll(jnp.where(row >= SUBLANES - s, prev, cur), s, 0)
                acc = acc + shifted * taps[s]
            conv_tiles.append(acc)
        conv = jnp.concatenate(conv_tiles, axis=0) if n_tiles > 1 else conv_tiles[0]
        conv16 = conv.astype(jnp.bfloat16)
        r_parts, i_parts = [], []
        for n in range(ct // LANES):
            ub = conv16[:, n * LANES:(n + 1) * LANES]
            r_parts.append(jnp.dot(ub, gaw_ref[n], preferred_element_type=jnp.float32))
            i_parts.append(jnp.dot(ub, gxw_ref[n], preferred_element_type=jnp.float32))
        r = _sigmoid(jnp.concatenate(r_parts, axis=1) + gab)
        gi = _sigmoid(jnp.concatenate(i_parts, axis=1) + gxb)
        log_a = r * log_a_scale
        th = jnp.tanh(log_a)
        mult = jnp.sqrt(-2.0 * th / (1.0 - th))
        a = jnp.exp(log_a)
        b = conv * gi * mult

        scanned = []
        for t in range(n_tiles):
            ac = a[t * SUBLANES:(t + 1) * SUBLANES, :]
            bc = b[t * SUBLANES:(t + 1) * SUBLANES, :]
            if t == 0 and pos0 == 0:
                start = (row == 0) & (r0 == 0)
                bc = jnp.where(start, conv[0:SUBLANES, :] * gi[0:SUBLANES, :], bc)
            for s in (1, 2, 4):
                a_sh = jnp.where(row >= s, pltpu.roll(ac, s, 0), 1.0)
                b_sh = jnp.where(row >= s, pltpu.roll(bc, s, 0), 0.0)
                bc = ac * b_sh + bc
                ac = ac * a_sh
            scanned.append((ac, bc))
        h_rows = []
        for ac, bc in scanned:
            h_rows.append(ac * h + bc)
            h = ac[SUBLANES - 1:SUBLANES, :] * h + bc[SUBLANES - 1:SUBLANES, :]
        h_seq = jnp.concatenate(h_rows, axis=0) if n_tiles > 1 else h_rows[0]
        g = g_ref[pl.ds(r0, rc), :]
        y_ref[pl.ds(r0, rc), :] = (h_seq * (g * _sigmoid(g))).astype(y_ref.dtype)
        return h

    hlast_ref[...] = lax.fori_loop(0, t_len // rc, chunk, h0_ref[...])


def _drop_ref(fn, pos):
    def body(*refs):
        return fn(*refs[:pos], *refs[pos + 1:])
    return body


def _lru_call(z, row_blk0, n_b, t_len, ct, u_col, g_col, cbuf, h0, cw, cb, gaw, gab, gxw, gxb, lam, pos0,
              y_rows, y_prev=None):
    lru_w = cw.shape[1]
    n_c = lru_w // ct
    nb_blk = ct // LANES
    rc = _pick_tile(t_len, (128, 64, 32, 16))
    kern = functools.partial(_lru_kernel, t_len=t_len, ct=ct, rc=rc, pos0=pos0)
    vec = lambda: pl.BlockSpec((1, ct), lambda b, c: (0, c))
    args = [z, z, cbuf, h0, cw, cb, gaw, gab, gxw, gxb, lam]
    extra_specs, aliases = [], {}
    if y_prev is not None:
        kern = _drop_ref(kern, len(args))
        extra_specs = [pl.BlockSpec(memory_space=pl.ANY)]
        aliases = {len(args): 0}
        args.append(y_prev)
    return pl.pallas_call(
        kern,
        grid=(n_b, n_c),
        input_output_aliases=aliases,
        in_specs=[
            pl.BlockSpec((t_len, ct), lambda b, c: (row_blk0 + b, u_col // ct + c)),
            pl.BlockSpec((t_len, ct), lambda b, c: (row_blk0 + b, g_col // ct + c)),
            pl.BlockSpec((None, SUBLANES, ct), lambda b, c: (b, 0, c)),
            pl.BlockSpec((None, 1, ct), lambda b, c: (b, 0, c)),
            pl.BlockSpec((CONV_W, ct), lambda b, c: (0, c)),
            vec(),
            pl.BlockSpec((nb_blk, LANES, LANES), lambda b, c: (c, 0, 0)),
            vec(),
            pl.BlockSpec((nb_blk, LANES, LANES), lambda b, c: (c, 0, 0)),
            vec(),
            vec(),
        ] + extra_specs,
        out_specs=[
            pl.BlockSpec((t_len, ct), lambda b, c: (row_blk0 + b, c)),
            pl.BlockSpec((None, SUBLANES, ct), lambda b, c: (b, 0, c)),
            pl.BlockSpec((None, 1, ct), lambda b, c: (b, 0, c)),
        ],
        out_shape=[
            jax.ShapeDtypeStruct((y_rows, lru_w), jnp.bfloat16),
            jax.ShapeDtypeStruct((n_b, SUBLANES, lru_w), jnp.float32),
            jax.ShapeDtypeStruct((n_b, 1, lru_w), jnp.float32),
        ],
        scratch_shapes=[
            pltpu.VMEM((t_len + SUBLANES, ct), jnp.float32),
        ],
        compiler_params=_params(("parallel", "parallel")),
        name="rg_lru",
    )(*args)


def _tree(x, op):
    tiles = [x[i:i + SUBLANES] for i in range(0, x.shape[0], SUBLANES)]
    while len(tiles) > 1:
        pairs = [op(tiles[i], tiles[i + 1]) for i in range(0, len(tiles) - 1, 2)]
        tiles = pairs + tiles[len(pairs) * 2:]
    return tiles[0]


def _key_to_float(key):
    return pltpu.bitcast(jnp.where(key < 0, key ^ 0x7FFFFFFF, key), jnp.float32)


def _kth_largest(count_ge, k, lanes):
    thr = jnp.where(count_ge(jnp.zeros((1, lanes), jnp.float32)) >= k, 0, INT_MIN).astype(jnp.int32)

    def bit_body(i, t):
        cand = t + lax.shift_left(jnp.int32(1), 30 - i)
        return jnp.where(count_ge(_key_to_float(cand)) >= k, cand, t)

    thr = lax.fori_loop(0, 31, bit_body, thr)
    return _key_to_float(jnp.maximum(thr, KEY_NEG_INF + 1))


def _n_key_chunks(last_q, s_all):
    n_allowed = jnp.minimum(
        lax.shift_left(lax.shift_right_logical(last_q, CHUNK_SHIFT) + 1, CHUNK_SHIFT), s_all)
    return lax.shift_right_logical(n_allowed + (KEY_BLOCK - 1), KEY_BLOCK_SHIFT)


def _attn_kernel(qi_ref, q_ref, ga_ref, wt_ref, k_ref, v_ref, ki_ref, y_ref,
                 kb_ref, vt_ref, kib_ref, qis_ref, qs_ref, sc_ref, m_ref, s_ref, ot_ref,
                 *, tq, n_sel, n_heads):
    kb = KEY_BLOCK
    group = n_heads // N_KV_HEADS
    s_all = sc_ref.shape[0]
    qb = pl.program_id(1)

    @pl.when(qb == 0)
    def _assemble_keys():
        for blk in range(s_all // LANES):
            rows = slice(blk * LANES, (blk + 1) * LANES)
            kib_ref[rows, :] = ki_ref[rows, :].astype(jnp.bfloat16)
            for g in range(N_KV_HEADS):
                cols = slice(g * HEAD_DIM, (g + 1) * HEAD_DIM)
                kb_ref[g, rows, :] = k_ref[rows, cols].astype(jnp.bfloat16)
                lane0 = (blk * LANES) % kb
                vt_ref[g, (blk * LANES) // kb, :, lane0:lane0 + LANES] = (
                    v_ref[rows, cols].T.astype(jnp.bfloat16))

    for h in range(N_IDX_HEADS):
        qis_ref[h * tq:(h + 1) * tq, :] = qi_ref[:, h * IDX_DIM:(h + 1) * IDX_DIM].astype(jnp.bfloat16)
    q_scale = LOG2_E * HEAD_DIM ** -0.5
    for h in range(n_heads):
        qs_ref[h * tq:(h + 1) * tq, :] = (
            q_ref[:, h * HEAD_DIM:(h + 1) * HEAD_DIM] * q_scale).astype(jnp.bfloat16)

    w = wt_ref[...] * (N_IDX_HEADS ** -0.5)
    qpos = qb * tq + lax.broadcasted_iota(jnp.int32, (1, tq), 1)
    qchunk = lax.shift_right_logical(qpos, CHUNK_SHIFT)
    n_chunks = _n_key_chunks(qb * tq + (tq - 1), s_all)

    def chunk_rows(c):
        return pl.ds(pl.multiple_of(c * kb, kb), kb)

    def score_chunk(c, carry):
        rows = chunk_rows(c)
        kic = kib_ref[rows, :]
        sc = jnp.zeros((kb, tq), jnp.float32)
        for hg in range(N_IDX_HEADS // 4):
            rel = lax.dot_general(kic, qis_ref[hg * 4 * tq:(hg + 1) * 4 * tq, :], _NT,
                                  preferred_element_type=jnp.float32)
            for j in range(4):
                h = hg * 4 + j
                sc = sc + jnp.maximum(rel[:, j * tq:(j + 1) * tq], 0.0) * w[h:h + 1, :]
        sc = sc * (IDX_DIM ** -0.5)
        kpos = c * kb + lax.broadcasted_iota(jnp.int32, (kb, tq), 0)
        allowed = lax.shift_right_logical(kpos, CHUNK_SHIFT) <= qchunk
        sc_ref[rows, :] = jnp.where(allowed, sc, -jnp.inf)
        return carry

    lax.fori_loop(0, n_chunks, score_chunk, 0)

    def count_ge(cand_f):
        def body(c, acc):
            return acc + _tree(jnp.where(sc_ref[chunk_rows(c), :] >= cand_f, 1.0, 0.0), jnp.add)
        acc = lax.fori_loop(0, n_chunks, body, jnp.zeros((SUBLANES, tq), jnp.float32))
        return jnp.sum(acc, axis=0, keepdims=True)

    thr_f = _kth_largest(count_ge, n_sel, tq)

    m_ref[...] = jnp.full(m_ref.shape, -jnp.inf, jnp.float32)
    s_ref[...] = jnp.zeros(s_ref.shape, jnp.float32)
    ot_ref[...] = jnp.zeros(ot_ref.shape, jnp.float32)
    slopes2 = [LOG2_E * 2.0 ** (-8.0 * (h + 1) / n_heads) for h in range(n_heads)]

    def attn_chunk(c, carry):
        rows = chunk_rows(c)
        kpos = c * kb + lax.broadcasted_iota(jnp.int32, (kb, tq), 0)
        dist = jnp.abs(qpos - kpos).astype(jnp.float32)
        dmc = jnp.where(sc_ref[rows, :] >= thr_f, dist, jnp.inf)
        for g in range(N_KV_HEADS):
            qg = qs_ref[g * group * tq:(g + 1) * group * tq, :]
            lg = lax.dot_general(kb_ref[g, rows, :], qg, _NT, preferred_element_type=jnp.float32)
            ps, alphas = [], []
            for j in range(group):
                h = g * group + j
                lj = lg[:, j * tq:(j + 1) * tq] - slopes2[h] * dmc
                m_old = m_ref[h:h + 1, :]
                m_new = jnp.maximum(m_old, jnp.max(_tree(lj, jnp.maximum), axis=0, keepdims=True))
                m_safe = jnp.where(m_new == -jnp.inf, 0.0, m_new)
                alpha = jnp.exp2(m_old - m_safe)
                p = jnp.exp2(lj - m_safe)
                srows = slice(h * SUBLANES, (h + 1) * SUBLANES)
                s_ref[srows, :] = s_ref[srows, :] * alpha + _tree(p, jnp.add)
                m_ref[h:h + 1, :] = m_new
                ps.append(p.astype(jnp.bfloat16))
                alphas.append(alpha)
            pv = jnp.dot(vt_ref[g, c], jnp.concatenate(ps, axis=1), preferred_element_type=jnp.float32)
            ot_ref[g] = ot_ref[g] * jnp.concatenate(alphas, axis=1) + pv
        return carry

    lax.fori_loop(0, n_chunks, attn_chunk, 0)

    for h in range(n_heads):
        g, j = divmod(h, group)
        s = jnp.sum(s_ref[h * SUBLANES:(h + 1) * SUBLANES, :], axis=0, keepdims=True)
        o = (ot_ref[g, :, j * tq:(j + 1) * tq] / s).T
        gate = ga_ref[:, h * HEAD_DIM:(h + 1) * HEAD_DIM]
        y_ref[:, h * HEAD_DIM:(h + 1) * HEAD_DIM] = (o * (gate * _sigmoid(gate))).astype(y_ref.dtype)


def _attn_call(z, ki, wt, y_prev, *, n_b, t_len, tq, cols):
    qi_col, q_col, ga_col, k_col, v_col = cols
    n_heads = (ga_col - q_col) // HEAD_DIM
    att_w = n_heads * HEAD_DIM
    qi_w = N_IDX_HEADS * IDX_DIM
    kv_w = N_KV_HEADS * HEAD_DIM
    n_q = t_len // tq
    n_sel = min(TOPK_MAX, t_len // 4)
    kern = functools.partial(_attn_kernel, tq=tq, n_sel=n_sel, n_heads=n_heads)
    qrow = lambda b, q: b * n_q + q
    n_args = 7
    return pl.pallas_call(
        _drop_ref(kern, n_args),
        grid=(n_b, n_q),
        input_output_aliases={n_args: 0},
        in_specs=[
            pl.BlockSpec((tq, qi_w), lambda b, q: (qrow(b, q), qi_col // qi_w)),
            pl.BlockSpec((tq, att_w), lambda b, q: (qrow(b, q), q_col // att_w)),
            pl.BlockSpec((tq, att_w), lambda b, q: (qrow(b, q), ga_col // att_w)),
            pl.BlockSpec((N_IDX_HEADS, tq), lambda b, q: (0, qrow(b, q))),
            pl.BlockSpec((t_len, kv_w), lambda b, q: (b, k_col // kv_w)),
            pl.BlockSpec((t_len, kv_w), lambda b, q: (b, v_col // kv_w)),
            pl.BlockSpec((t_len, IDX_DIM), lambda b, q: (b, 0)),
            pl.BlockSpec(memory_space=pl.ANY),
        ],
        out_specs=pl.BlockSpec((tq, att_w), lambda b, q: (qrow(b, q), 0)),
        out_shape=jax.ShapeDtypeStruct(y_prev.shape, y_prev.dtype),
        scratch_shapes=[
            pltpu.VMEM((N_KV_HEADS, t_len, HEAD_DIM), jnp.bfloat16),
            pltpu.VMEM((N_KV_HEADS, t_len // KEY_BLOCK, HEAD_DIM, KEY_BLOCK), jnp.bfloat16),
            pltpu.VMEM((t_len, IDX_DIM), jnp.bfloat16),
            pltpu.VMEM((N_IDX_HEADS * tq, IDX_DIM), jnp.bfloat16),
            pltpu.VMEM((n_heads * tq, HEAD_DIM), jnp.bfloat16),
            pltpu.VMEM((t_len, tq), jnp.float32),
            pltpu.VMEM((n_heads, tq), jnp.float32),
            pltpu.VMEM((n_heads * SUBLANES, tq), jnp.float32),
            pltpu.VMEM((N_KV_HEADS, HEAD_DIM, (n_heads // N_KV_HEADS) * tq), jnp.float32),
        ],
        compiler_params=_params(("arbitrary", "arbitrary")),
        name="sparse_attn",
    )(z, z, z, wt, z, z, ki, y_prev)


def _decode_attn_kernel(qi_ref, q_ref, ga_ref, wrow_ref, kcur_ref, vcur_ref, kicur_ref,
                        kp_ref, vp_ref, kip_ref, y_ref,
                        kb_ref, vt_ref, kib_ref, qit_ref, bdt_ref, sc_ref, stage_ref,
                        *, t_q, n_past, pos0, n_sel, n_heads):
    kb = KEY_BLOCK
    group = n_heads // N_KV_HEADS
    hpt = LANES // t_q
    n_tiles = n_heads // hpt
    gpt = hpt // group
    s_all = n_past + t_q
    s_pad = sc_ref.shape[0]
    kv_w = N_KV_HEADS * HEAD_DIM
    tile_w = gpt * HEAD_DIM
    slot_shifts = [LANES >> (i + 1) for i in range(hpt.bit_length() - 1)]

    def sum_over_slots(x):
        for sh in slot_shifts:
            x = x + pltpu.roll(x, sh, 1)
        return x

    def load_block(k_src, v_src, ki_src, src_row, dst_row, heads_on_rows):
        rows = slice(src_row, src_row + LANES)
        dst = slice(dst_row, dst_row + LANES)
        kib_ref[dst, :] = ki_src[rows, 0:IDX_DIM].astype(jnp.bfloat16)
        for g in range(N_KV_HEADS):
            if heads_on_rows:
                idx = (pl.ds(src_row * N_KV_HEADS + g, LANES, stride=N_KV_HEADS), slice(None))
            else:
                idx = (rows, slice(g * HEAD_DIM, (g + 1) * HEAD_DIM))
            kb_ref[dst, g * HEAD_DIM:(g + 1) * HEAD_DIM] = k_src[idx].astype(jnp.bfloat16)
            lane0 = dst_row % kb
            vt_ref[g // gpt, dst_row // kb, (g % gpt) * HEAD_DIM:(g % gpt + 1) * HEAD_DIM,
                   lane0:lane0 + LANES] = v_src[idx].T.astype(jnp.bfloat16)

    for blk in range(n_past // LANES):
        load_block(kp_ref, vp_ref, kip_ref, blk * LANES, blk * LANES, True)
    stage_ref[...] = jnp.zeros(stage_ref.shape, jnp.float32)
    stage_ref[0, 0:t_q, 0:kv_w] = kcur_ref[...]
    stage_ref[1, 0:t_q, 0:kv_w] = vcur_ref[...]
    stage_ref[2, 0:t_q, 0:IDX_DIM] = kicur_ref[...]
    for blk in range((s_pad - n_past) // LANES):
        load_block(stage_ref.at[0], stage_ref.at[1], stage_ref.at[2], blk * LANES, n_past + blk * LANES,
                   False)

    for h in range(N_IDX_HEADS):
        qit_ref[h * t_q:(h + 1) * t_q, :] = qi_ref[:, h * IDX_DIM:(h + 1) * IDX_DIM].astype(jnp.bfloat16)
    bdt_ref[...] = jnp.zeros(bdt_ref.shape, bdt_ref.dtype)
    q_scale = LOG2_E * HEAD_DIM ** -0.5
    for h in range(n_heads):
        tile, slot = divmod(h, hpt)
        gl = slot // group
        bdt_ref[tile, slot * t_q:(slot + 1) * t_q, gl * HEAD_DIM:(gl + 1) * HEAD_DIM] = (
            q_ref[:, h * HEAD_DIM:(h + 1) * HEAD_DIM] * q_scale).astype(jnp.bfloat16)

    lane = lax.broadcasted_iota(jnp.int32, (1, LANES), 1)
    slot_l = lax.shift_right_logical(lane, t_q.bit_length() - 1)
    qpos = pos0 + (lane & (t_q - 1))
    qchunk = lax.shift_right_logical(qpos, CHUNK_SHIFT)
    w = wrow_ref[...] * (N_IDX_HEADS ** -0.5)
    n_chunks = s_pad // kb

    for c in range(n_chunks):
        rows = slice(c * kb, (c + 1) * kb)
        rel = lax.dot_general(kib_ref[rows, :], qit_ref[...], _NT, preferred_element_type=jnp.float32)
        x = jnp.maximum(rel, 0.0) * w
        acc = x[:, 0:LANES]
        for i in range(1, x.shape[1] // LANES):
            acc = acc + x[:, i * LANES:(i + 1) * LANES]
        sc = sum_over_slots(acc) * (IDX_DIM ** -0.5)
        kpos = c * kb + lax.broadcasted_iota(jnp.int32, (kb, LANES), 0)
        allowed = (lax.shift_right_logical(kpos, CHUNK_SHIFT) <= qchunk) & (kpos < s_all)
        sc_ref[rows, :] = jnp.where(allowed, sc, -jnp.inf)

    def count_ge(cand_f):
        return jnp.sum(_tree(jnp.where(sc_ref[...] >= cand_f, 1.0, 0.0), jnp.add), axis=0, keepdims=True)

    thr_f = _kth_largest(count_ge, n_sel, LANES)

    slopes2 = []
    for p in range(n_tiles):
        head = (p * hpt + slot_l + 1).astype(jnp.float32)
        slopes2.append(LOG2_E * jnp.exp2(head * (-8.0 / n_heads)))
    m = [jnp.full((1, LANES), -jnp.inf, jnp.float32) for _ in range(n_tiles)]
    s8 = [jnp.zeros((SUBLANES, LANES), jnp.float32) for _ in range(n_tiles)]
    ot = [jnp.zeros((tile_w, LANES), jnp.float32) for _ in range(n_tiles)]
    for c in range(n_chunks):
        rows = slice(c * kb, (c + 1) * kb)
        kpos = c * kb + lax.broadcasted_iota(jnp.int32, (kb, LANES), 0)
        dist = jnp.abs(qpos - kpos).astype(jnp.float32)
        dmc = jnp.where(sc_ref[rows, :] >= thr_f, dist, jnp.inf)
        for p in range(n_tiles):
            lg = lax.dot_general(kb_ref[rows, p * tile_w:(p + 1) * tile_w], bdt_ref[p], _NT,
                                 preferred_element_type=jnp.float32)
            lj = lg - slopes2[p] * dmc
            m_new = jnp.maximum(m[p], jnp.max(_tree(lj, jnp.maximum), axis=0, keepdims=True))
            m_safe = jnp.where(m_new == -jnp.inf, 0.0, m_new)
            alpha = jnp.exp2(m[p] - m_safe)
            pr = jnp.exp2(lj - m_safe)
            s8[p] = s8[p] * alpha + _tree(pr, jnp.add)
            m[p] = m_new
            pv = jnp.dot(vt_ref[p, c], pr.astype(jnp.bfloat16), preferred_element_type=jnp.float32)
            ot[p] = ot[p] * alpha + pv

    for p in range(n_tiles):
        o = ot[p] / jnp.sum(s8[p], axis=0, keepdims=True)
        for gl in range(gpt):
            o_t = o[gl * HEAD_DIM:(gl + 1) * HEAD_DIM, :].T
            for j in range(group):
                slot = gl * group + j
                h = p * hpt + slot
                gate = ga_ref[:, h * HEAD_DIM:(h + 1) * HEAD_DIM]
                y_ref[:, h * HEAD_DIM:(h + 1) * HEAD_DIM] = (
                    o_t[slot * t_q:(slot + 1) * t_q, :] * (gate * _sigmoid(gate))).astype(y_ref.dtype)


def _decode_attn_call(z, ki, wrow, past, y_prev, *, layer, row_blk0, n_b, t_q, pos0, cols):
    qi_col, q_col, ga_col, k_col, v_col = cols
    n_heads = (ga_col - q_col) // HEAD_DIM
    att_w = n_heads * HEAD_DIM
    qi_w = N_IDX_HEADS * IDX_DIM
    kv_w = N_KV_HEADS * HEAD_DIM
    group = n_heads // N_KV_HEADS
    hpt = LANES // t_q
    assert t_q & (t_q - 1) == 0 and LANES % t_q == 0 and n_heads % hpt == 0 and hpt % group == 0
    assert (N_IDX_HEADS * t_q) % LANES == 0
    n_past = past[2].shape[2]
    s_all = n_past + t_q
    cur_pad = ((t_q + KEY_BLOCK - 1) // KEY_BLOCK) * KEY_BLOCK
    s_pad = n_past + cur_pad
    n_sel = min(TOPK_MAX, s_all // 4)
    n_tiles = n_heads // hpt
    gpt = hpt // group
    kern = functools.partial(_decode_attn_kernel, t_q=t_q, n_past=n_past, pos0=pos0, n_sel=n_sel,
                             n_heads=n_heads)
    row = lambda b: row_blk0 + b
    n_args = 10
    return pl.pallas_call(
        _drop_ref(kern, n_args),
        grid=(n_b,),
        input_output_aliases={n_args: 0},
        in_specs=[
            pl.BlockSpec((t_q, qi_w), lambda b: (row(b), qi_col // qi_w)),
            pl.BlockSpec((t_q, att_w), lambda b: (row(b), q_col // att_w)),
            pl.BlockSpec((t_q, att_w), lambda b: (row(b), ga_col // att_w)),
            pl.BlockSpec((None, 1, N_IDX_HEADS * t_q), lambda b: (b, 0, 0)),
            pl.BlockSpec((t_q, kv_w), lambda b: (row(b), k_col // kv_w)),
            pl.BlockSpec((t_q, kv_w), lambda b: (row(b), v_col // kv_w)),
            pl.BlockSpec((t_q, IDX_DIM), lambda b: (row(b), 0)),
            pl.BlockSpec((None, None, n_past * N_KV_HEADS, HEAD_DIM), lambda b: (layer, b, 0, 0)),
            pl.BlockSpec((None, None, n_past * N_KV_HEADS, HEAD_DIM), lambda b: (layer, b, 0, 0)),
            pl.BlockSpec((None, None, n_past, IDX_DIM), lambda b: (layer, b, 0, 0)),
            pl.BlockSpec(memory_space=pl.ANY),
        ],
        out_specs=pl.BlockSpec((t_q, att_w), lambda b: (row(b), 0)),
        out_shape=jax.ShapeDtypeStruct(y_prev.shape, y_prev.dtype),
        scratch_shapes=[
            pltpu.VMEM((s_pad, kv_w), jnp.bfloat16),
            pltpu.VMEM((N_KV_HEADS // gpt, s_pad // KEY_BLOCK, gpt * HEAD_DIM, KEY_BLOCK), jnp.bfloat16),
            pltpu.VMEM((s_pad, IDX_DIM), jnp.bfloat16),
            pltpu.VMEM((N_IDX_HEADS * t_q, IDX_DIM), jnp.bfloat16),
            pltpu.VMEM((n_tiles, LANES, gpt * HEAD_DIM), jnp.bfloat16),
            pltpu.VMEM((s_pad, LANES), jnp.float32),
            pltpu.VMEM((3, cur_pad, kv_w), jnp.float32),
        ],
        compiler_params=_params(("parallel",)),
        name="decode_attn",
    )(z, z, z, wrow, z, z, ki, *past, y_prev)


def _out_kernel(yl_ref, ya_ref, xb_ref, p_ref, xf_ref, wo_ref, wg_ref, wp_ref, o_ref, *, alpha):
    half = yl_ref.shape[1]
    mix = (jnp.dot(yl_ref[...], wo_ref[0:half, :], preferred_element_type=jnp.float32)
           + jnp.dot(ya_ref[...], wo_ref[half:, :], preferred_element_type=jnp.float32))
    gate = jax.nn.sigmoid(jnp.dot(xb_ref[...], wg_ref[...], preferred_element_type=jnp.float32))
    emb = jnp.dot(p_ref[...], wp_ref[...], preferred_element_type=jnp.float32)
    o_ref[...] = alpha * xf_ref[...] + mix + gate * emb


def _out_call(yl, ya, xb, p, xf, wo, wg, wp, layer, alpha, tm, tn):
    m, d = xf.shape
    half = yl.shape[1]
    ple = p.shape[1]
    row = lambda w: pl.BlockSpec((tm, w), lambda i, j: (i, 0))
    col = lambda k: pl.BlockSpec((k, tn), lambda i, j: (0, j))
    lcol = lambda k: pl.BlockSpec((None, k, tn), lambda i, j: (layer, 0, j))
    return pl.pallas_call(
        functools.partial(_out_kernel, alpha=alpha),
        grid=(m // tm, d // tn),
        in_specs=[row(half), row(ya.shape[1]), row(d), row(ple),
                  pl.BlockSpec((tm, tn), lambda i, j: (i, j)),
                  lcol(wo.shape[1]), lcol(d), col(ple)],
        out_specs=pl.BlockSpec((tm, tn), lambda i, j: (i, j)),
        out_shape=jax.ShapeDtypeStruct((m, d), jnp.float32),
        compiler_params=_params(("parallel", "parallel")),
        name="out_proj",
    )(yl, ya, xb, p, xf, wo, wg, wp)


def _ln_rows(x_ref, g_ref, b_ref, out_refs, rs):
    def sub_block(c, carry):
        rows = pl.ds(pl.multiple_of(c * rs, rs), rs)
        x = x_ref[rows, :]
        mu = jnp.mean(x, axis=-1, keepdims=True)
        xc = x - mu
        var = jnp.mean(xc * xc, axis=-1, keepdims=True)
        y = xc * lax.rsqrt(var + LN_EPS) * g_ref[...] + b_ref[...]
        for o_ref in out_refs:
            o_ref[rows, :] = y.astype(o_ref.dtype)
        return carry

    lax.fori_loop(0, x_ref.shape[0] // rs, sub_block, 0)


def _ln_kernel(x_ref, g_ref, b_ref, of_ref, ob_ref, *, rs):
    _ln_rows(x_ref, g_ref, b_ref, (of_ref, ob_ref), rs)


def _ln_split_kernel(x_ref, g_ref, b_ref, head_ref, tail_ref, *, rs, n_head):
    i = pl.program_id(0)

    @pl.when(i < n_head)
    def _head():
        _ln_rows(x_ref, g_ref, b_ref, (head_ref,), rs)

    @pl.when(i >= n_head)
    def _tail():
        _ln_rows(x_ref, g_ref, b_ref, (tail_ref,), rs)


def _ln_split_call(x, g, b, tr, m_head):
    m, d = x.shape
    n_head = m_head // tr
    return pl.pallas_call(
        functools.partial(_ln_split_kernel, rs=_pick_tile(tr, (32, 16)), n_head=n_head),
        grid=(m // tr,),
        in_specs=[pl.BlockSpec((tr, d), lambda i: (i, 0)),
                  pl.BlockSpec((1, d), lambda i: (0, 0)),
                  pl.BlockSpec((1, d), lambda i: (0, 0))],
        out_specs=[pl.BlockSpec((tr, d), lambda i: (jnp.minimum(i, n_head - 1), 0)),
                   pl.BlockSpec((tr, d), lambda i: (jnp.maximum(i - n_head, 0), 0))],
        out_shape=[jax.ShapeDtypeStruct((m_head, d), jnp.float32),
                   jax.ShapeDtypeStruct((m - m_head, d), jnp.float32)],
        compiler_params=_params(("arbitrary",)),
        name="layer_norm_out",
    )(x, g, b)


def _ln_call(x, g, b, tr):
    m, d = x.shape
    return pl.pallas_call(
        functools.partial(_ln_kernel, rs=_pick_tile(tr, (32, 16))),
        grid=(m // tr,),
        in_specs=[pl.BlockSpec((tr, d), lambda i: (i, 0)),
                  pl.BlockSpec((1, d), lambda i: (0, 0)),
                  pl.BlockSpec((1, d), lambda i: (0, 0))],
        out_specs=[pl.BlockSpec((tr, d), lambda i: (i, 0)),
                   pl.BlockSpec((tr, d), lambda i: (i, 0))],
        out_shape=[jax.ShapeDtypeStruct((m, d), jnp.float32),
                   jax.ShapeDtypeStruct((m, d), jnp.bfloat16)],
        compiler_params=_params(("parallel",)),
        name="layer_norm",
    )(x, g, b)


def kernel(x_prompt, x_sample, cache_k, cache_v, cache_kidx, state_conv, state_h, p_prompt, p_sample, w_in, conv_w, conv_b, gate_a_w, gate_a_b, gate_x_w, gate_x_b, lru_lambda, w_out, ln_g, ln_b, ple_proj, ple_gate):
    bf16 = jnp.bfloat16
    n_bp, t_p, d = x_prompt.shape
    n_bs, t_s, _ = x_sample.shape
    depth = w_in.shape[0]
    n_past = cache_k.shape[2]
    lru_w = conv_w.shape[2]
    att_w = d - lru_w
    n_heads = att_w // HEAD_DIM
    kv_w = N_KV_HEADS * HEAD_DIM
    qi_w = N_IDX_HEADS * IDX_DIM
    alpha = (2 * depth) ** 0.25
    m_p, m_s = n_bp * t_p, n_bs * t_s
    m = m_p + m_s
    assert t_p % KEY_BLOCK == 0 and m_p % t_s == 0 and t_s % SUBLANES == 0 and n_past % KEY_BLOCK == 0
    assert lru_w // N_LRU_BLOCKS == LANES and CHUNK == 1 << CHUNK_SHIFT

    o_u, o_gl = 0, lru_w
    o_q = 2 * lru_w
    o_k = o_q + att_w
    o_v = o_k + kv_w
    o_ga = o_v + kv_w
    o_qi = o_ga + att_w
    o_ki = o_qi + qi_w
    c_qi, c_q = 0, qi_w
    c_ga = c_q + att_w
    c_u = c_ga + att_w
    c_gl = c_u + lru_w
    c_k = c_gl + lru_w
    c_v = c_k + kv_w
    n_main = c_v + kv_w

    tm = _pick_tile(m, (768, 512, 384, 256, 128))
    tm_in = _pick_tile(m, (1056, 768, 512, 384, 256, 128))
    tn_in = _pick_tile(n_main, (1024, 512))
    tn_out = _pick_tile(d, (512,))
    tr_ln = _pick_tile(m, (256, 128))
    tr_out = [c for c in (256, 128, 64, 32, 16, 8) if m_p % c == 0 and m_s % c == 0][0]
    ct_p = _pick_tile(lru_w, (512, 256))
    tq = _pick_tile(t_p, (256,))

    xf, xb = _pack_rows_call(x_prompt.reshape(m_p, d), x_sample.reshape(m_s, d), tr_out)
    hist_pad = ((0, 0), (SUBLANES - (CONV_W - 1), 0), (0, 0))
    zero_conv = jnp.zeros((n_bp, SUBLANES, lru_w), jnp.float32)
    zero_h = jnp.zeros((n_bp, 1, lru_w), jnp.float32)
    past = (cache_k.reshape(depth, n_bs, n_past * N_KV_HEADS, HEAD_DIM),
            cache_v.reshape(depth, n_bs, n_past * N_KV_HEADS, HEAD_DIM), cache_kidx)

    yl = jnp.zeros((m, lru_w), bf16)
    ya = jnp.zeros((m, att_w), bf16)
    cast_rows = _pick_tile(d, (256, 128))
    w_out16 = _cast_call(w_out, cast_rows)
    ple_gate16 = _cast_call(ple_gate, cast_rows)
    outs_p, outs_s = [], []
    for i in range(depth):
        wl = w_in[i]
        seg = lambda o, n: wl[:, o:o + n]
        w_main = jnp.concatenate(
            [seg(o_qi, qi_w), seg(o_q, att_w), seg(o_ga, att_w), seg(o_u, lru_w), seg(o_gl, lru_w),
             seg(o_k, kv_w), seg(o_v, kv_w)], axis=1).astype(bf16)
        w_side = wl[:, o_ki:].astype(bf16)

        z, k_rows, v_rows = _in_proj_call(xb, w_main, tm_in, tn_in, c_k)
        ki, wi = _idx_proj_call(xb, w_side, tm)
        wt_p = wi[:m_p].T
        wrow_s = wi[m_p:].reshape(n_bs, t_s, N_IDX_HEADS).transpose(0, 2, 1).reshape(n_bs, 1, -1)

        lru_args = (conv_w[i], conv_b[i][None], gate_a_w[i].astype(bf16), gate_a_b[i][None],
                    gate_x_w[i].astype(bf16), gate_x_b[i][None], lru_lambda[i][None])
        yl, conv_p, h_p = _lru_call(z, 0, n_bp, t_p, ct_p, c_u, c_gl, zero_conv, zero_h, *lru_args,
                                    pos0=0, y_rows=m, y_prev=yl)
        yl, conv_s, h_s = _lru_call(z, m_p // t_s, n_bs, t_s, lru_w, c_u, c_gl,
                                    jnp.pad(state_conv[i], hist_pad), state_h[i][:, None, :], *lru_args,
                                    pos0=n_past, y_rows=m, y_prev=yl)
        conv_p, conv_s = conv_p[:, :CONV_W - 1], conv_s[:, :CONV_W - 1]

        cols = (c_qi, c_q, c_ga, c_k, c_v)
        ya = _attn_call(z, ki, wt_p, ya, n_b=n_bp, t_len=t_p, tq=tq, cols=cols)
        ya = _decode_attn_call(z, ki, wrow_s, past, ya, layer=i, row_blk0=m_p // t_s, n_b=n_bs, t_q=t_s,
                               pos0=n_past, cols=cols)

        p = jnp.concatenate([p_prompt[i].reshape(m_p, -1), p_sample[i].reshape(m_s, -1)], axis=0).astype(bf16)
        pre = _out_call(yl, ya, xb, p, xf, w_out16, ple_gate16, ple_proj[i].astype(bf16), i, alpha, tm,
                        tn_out)
        if i + 1 < depth:
            xf, xb = _ln_call(pre, ln_g[i][None], ln_b[i][None], tr_ln)
        else:
            y_p, y_s = _ln_split_call(pre, ln_g[i][None], ln_b[i][None], tr_out, m_p)

        hp = m_p * N_KV_HEADS
        outs_p.append((k_rows[:hp].reshape(n_bp, t_p, N_KV_HEADS, HEAD_DIM),
                       v_rows[:hp].reshape(n_bp, t_p, N_KV_HEADS, HEAD_DIM),
                       ki[:m_p].reshape(n_bp, t_p, IDX_DIM), conv_p, h_p[:, 0, :]))
        outs_s.append((k_rows[hp:].reshape(n_bs, t_s, N_KV_HEADS, HEAD_DIM),
                       v_rows[hp:].reshape(n_bs, t_s, N_KV_HEADS, HEAD_DIM),
                       ki[m_p:].reshape(n_bs, t_s, IDX_DIM), conv_s, h_s[:, 0, :]))

    stack = lambda outs, j: jnp.stack([o[j] for o in outs])
    return (y_p.reshape(n_bp, t_p, d), y_s.reshape(n_bs, t_s, d),
            stack(outs_p, 0), stack(outs_p, 1), stack(outs_p, 2), stack(outs_p, 3), stack(outs_p, 4),
            stack(outs_s, 0), stack(outs_s, 1), stack(outs_s, 2), stack(outs_s, 3), stack(outs_s, 4))
```

```python
import functools

import jax
import jax.numpy as jnp
from jax import lax
from jax.experimental import pallas as pl
from jax.experimental.pallas import tpu as pltpu

CHUNK = 64
CHUNK_SHIFT = 6
N_LRU_BLOCKS = 16
CONV_W = 4
LRU_C = 8.0
HEAD_DIM = 128
N_KV_HEADS = 4
N_IDX_HEADS = 32
IDX_DIM = 128
TOPK_MAX = 256
LN_EPS = 1e-5
LOG2_E = 1.4426950408889634

LANES = 128
SUBLANES = 8
VMEM_LIMIT_BYTES = 60 * 1024 * 1024

KEY_BLOCK = 256
KEY_BLOCK_SHIFT = 8

INT_MIN = -(2 ** 31)
KEY_NEG_INF = -2139095041

_NT = (((1,), (1,)), ((), ()))


def _pick_tile(n, candidates):
    for c in candidates:
        if n % c == 0:
            return c
    return n


def _params(sem):
    return pltpu.CompilerParams(dimension_semantics=sem, vmem_limit_bytes=VMEM_LIMIT_BYTES)


def _pack_rows_kernel(head_ref, tail_ref, of_ref, ob_ref, *, n_head):
    i = pl.program_id(0)

    @pl.when(i < n_head)
    def _head():
        of_ref[...] = head_ref[...]
        ob_ref[...] = head_ref[...].astype(ob_ref.dtype)

    @pl.when(i >= n_head)
    def _tail():
        of_ref[...] = tail_ref[...]
        ob_ref[...] = tail_ref[...].astype(ob_ref.dtype)


def _pack_rows_call(head, tail, tr):
    (m_h, d), m_t = head.shape, tail.shape[0]
    n_head = m_h // tr
    out_spec = pl.BlockSpec((tr, d), lambda i: (i, 0))
    return pl.pallas_call(
        functools.partial(_pack_rows_kernel, n_head=n_head),
        grid=((m_h + m_t) // tr,),
        in_specs=[pl.BlockSpec((tr, d), lambda i: (jnp.minimum(i, n_head - 1), 0)),
                  pl.BlockSpec((tr, d), lambda i: (jnp.maximum(i - n_head, 0), 0))],
        out_specs=[out_spec, out_spec],
        out_shape=[jax.ShapeDtypeStruct((m_h + m_t, d), jnp.float32),
                   jax.ShapeDtypeStruct((m_h + m_t, d), jnp.bfloat16)],
        compiler_params=_params(("parallel",)),
        name="pack_rows",
    )(head, tail)


def _cast_kernel(w_ref, o_ref, *, rs):
    def piece(c, carry):
        rows = pl.ds(pl.multiple_of(c * rs, rs), rs)
        o_ref[rows, :] = w_ref[rows, :].astype(o_ref.dtype)
        return carry

    lax.fori_loop(0, w_ref.shape[0] // rs, piece, 0)


def _cast_call(w, rows):
    depth, k, n = w.shape
    spec = pl.BlockSpec((None, rows, n), lambda l, i: (l, i, 0))
    return pl.pallas_call(
        functools.partial(_cast_kernel, rs=_pick_tile(rows, (32, 16))),
        grid=(depth, k // rows),
        in_specs=[spec],
        out_specs=spec,
        out_shape=jax.ShapeDtypeStruct(w.shape, jnp.bfloat16),
        compiler_params=_params(("parallel", "parallel")),
        name="cast_bf16",
    )(w)


def _in_proj_kernel(x_ref, w_ref, z_ref, k_ref, v_ref, *, j_kv):
    z_ref[...] = jnp.dot(x_ref[...], w_ref[...], preferred_element_type=jnp.float32)

    @pl.when(pl.program_id(1) == j_kv)
    def _emit_kv():
        tm = z_ref.shape[0]
        kv_w = N_KV_HEADS * HEAD_DIM
        for g in range(N_KV_HEADS):
            rows = pl.ds(g, tm, stride=N_KV_HEADS)
            k_ref[rows, :] = z_ref[:, g * HEAD_DIM:(g + 1) * HEAD_DIM]
            v_ref[rows, :] = z_ref[:, kv_w + g * HEAD_DIM:kv_w + (g + 1) * HEAD_DIM]


def _in_proj_call(x, w, tm, tn, k_col):
    m, kdim = x.shape
    n = w.shape[1]
    assert k_col % tn == 0 and tn == 2 * N_KV_HEADS * HEAD_DIM
    kv_shape = jax.ShapeDtypeStruct((m * N_KV_HEADS, HEAD_DIM), jnp.float32)
    kv_spec = pl.BlockSpec((tm * N_KV_HEADS, HEAD_DIM), lambda i, j: (i, 0))
    return pl.pallas_call(
        functools.partial(_in_proj_kernel, j_kv=k_col // tn),
        grid=(m // tm, n // tn),
        in_specs=[pl.BlockSpec((tm, kdim), lambda i, j: (i, 0)),
                  pl.BlockSpec((kdim, tn), lambda i, j: (0, j))],
        out_specs=[pl.BlockSpec((tm, tn), lambda i, j: (i, j)), kv_spec, kv_spec],
        out_shape=[jax.ShapeDtypeStruct((m, n), jnp.float32), kv_shape, kv_shape],
        compiler_params=_params(("parallel", "arbitrary")),
        name="in_proj",
    )(x, w)


def _idx_proj_kernel(x_ref, w_ref, ki_ref, wi_ref):
    r = jnp.dot(x_ref[...], w_ref[...], preferred_element_type=jnp.float32)
    ki_ref[...] = r[:, 0:IDX_DIM]
    wi_ref[...] = r[:, IDX_DIM:]


def _idx_proj_call(x, w, tm):
    m, kdim = x.shape
    n = w.shape[1]
    return pl.pallas_call(
        _idx_proj_kernel,
        grid=(m // tm,),
        in_specs=[pl.BlockSpec((tm, kdim), lambda i: (i, 0)),
                  pl.BlockSpec((kdim, n), lambda i: (0, 0))],
        out_specs=[pl.BlockSpec((tm, IDX_DIM), lambda i: (i, 0)),
                   pl.BlockSpec((tm, n - IDX_DIM), lambda i: (i, 0))],
        out_shape=[jax.ShapeDtypeStruct((m, IDX_DIM), jnp.float32),
                   jax.ShapeDtypeStruct((m, n - IDX_DIM), jnp.float32)],
        compiler_params=_params(("parallel",)),
        name="idx_proj",
    )(x, w)


def _sigmoid(x):
    return 0.5 * jnp.tanh(0.5 * x) + 0.5


def _lru_kernel(u_ref, g_ref, cbuf_ref, h0_ref, cw_ref, cb_ref, gaw_ref, gab_ref,
                gxw_ref, gxb_ref, lam_ref, y_ref, cnew_ref, hlast_ref,
                full_ref, *, t_len, ct, rc, pos0):
    full_ref[0:SUBLANES, :] = cbuf_ref[...]
    full_ref[SUBLANES:SUBLANES + t_len, :] = u_ref[...]
    cnew_ref[...] = pltpu.roll(full_ref[t_len:t_len + SUBLANES, :], CONV_W - 1, 0)

    n_tiles = rc // SUBLANES
    row = lax.broadcasted_iota(jnp.int32, (SUBLANES, ct), 0)
    taps = [jnp.broadcast_to(cw_ref[CONV_W - 1 - s:CONV_W - s, :], (SUBLANES, ct)) for s in range(CONV_W)]
    cb = jnp.broadcast_to(cb_ref[...], (SUBLANES, ct))
    nlam = -lam_ref[...]
    softplus = jnp.maximum(nlam, 0.0) + jnp.log1p(jnp.exp(-jnp.abs(nlam)))
    log_a_scale = -LRU_C * softplus
    gab = gab_ref[...]
    gxb = gxb_ref[...]

    def chunk(c, h):
        r0 = pl.multiple_of(c * rc, rc)
        conv_tiles = []
        for t in range(n_tiles):
            prev = full_ref[pl.ds(r0 + t * SUBLANES, SUBLANES), :]
            cur = full_ref[pl.ds(r0 + (t + 1) * SUBLANES, SUBLANES), :]
            acc = cb + cur * taps[0]
            for s in range(1, CONV_W):
                shifted = pltpu.roll(jnp.where(row >= SUBLANES - s, prev, cur), s, 0)
                acc = acc + shifted * taps[s]
            conv_tiles.append(acc)
        conv = jnp.concatenate(conv_tiles, axis=0) if n_tiles > 1 else conv_tiles[0]
        conv16 = conv.astype(jnp.bfloat16)
        r_parts, i_parts = [], []
        for n in range(ct // LANES):
            ub = conv16[:, n * LANES:(n + 1) * LANES]
            r_parts.append(jnp.dot(ub, gaw_ref[n], preferred_element_type=jnp.float32))
            i_parts.append(jnp.dot(ub, gxw_ref[n], preferred_element_type=jnp.float32))
        r = _sigmoid(jnp.concatenate(r_parts, axis=1) + gab)
        gi = _sigmoid(jnp.concatenate(i_parts, axis=1) + gxb)
        log_a = r * log_a_scale
        th = jnp.tanh(log_a)
        mult = jnp.sqrt(-2.0 * th / (1.0 - th))
        a = jnp.exp(log_a)
        b = conv * gi * mult

        scanned = []
        for t in range(n_tiles):
            ac = a[t * SUBLANES:(t + 1) * SUBLANES, :]
            bc = b[t * SUBLANES:(t + 1) * SUBLANES, :]
            if t == 0 and pos0 == 0:
                start = (row == 0) & (r0 == 0)
                bc = jnp.where(start, conv[0:SUBLANES, :] * gi[0:SUBLANES, :], bc)
            for s in (1, 2, 4):
                a_sh = jnp.where(row >= s, pltpu.roll(ac, s, 0), 1.0)
                b_sh = jnp.where(row >= s, pltpu.roll(bc, s, 0), 0.0)
                bc = ac * b_sh + bc
                ac = ac * a_sh
            scanned.append((ac, bc))
        h_rows = []
        for ac, bc in scanned:
            h_rows.append(ac * h + bc)
            h = ac[SUBLANES - 1:SUBLANES, :] * h + bc[SUBLANES - 1:SUBLANES, :]
        h_seq = jnp.concatenate(h_rows, axis=0) if n_tiles > 1 else h_rows[0]
        g = g_ref[pl.ds(r0, rc), :]
        y_ref[pl.ds(r0, rc), :] = (h_seq * (g * _sigmoid(g))).astype(y_ref.dtype)
        return h

    hlast_ref[...] = lax.fori_loop(0, t_len // rc, chunk, h0_ref[...])


def _drop_ref(fn, pos):
    def body(*refs):
        return fn(*refs[:pos], *refs[pos + 1:])
    return body


def _lru_call(z, row_blk0, n_b, t_len, ct, u_col, g_col, cbuf, h0, cw, cb, gaw, gab, gxw, gxb, lam, pos0,
              y_rows, y_prev=None):
    lru_w = cw.shape[1]
    n_c = lru_w // ct
    nb_blk = ct // LANES
    rc = _pick_tile(t_len, (128, 64, 32, 16))
    kern = functools.partial(_lru_kernel, t_len=t_len, ct=ct, rc=rc, pos0=pos0)
    vec = lambda: pl.BlockSpec((1, ct), lambda b, c: (0, c))
    args = [z, z, cbuf, h0, cw, cb, gaw, gab, gxw, gxb, lam]
    extra_specs, aliases = [], {}
    if y_prev is not None:
        kern = _drop_ref(kern, len(args))
        extra_specs = [pl.BlockSpec(memory_space=pl.ANY)]
        aliases = {len(args): 0}
        args.append(y_prev)
    return pl.pallas_call(
        kern,
        grid=(n_b, n_c),
        input_output_aliases=aliases,
        in_specs=[
            pl.BlockSpec((t_len, ct), lambda b, c: (row_blk0 + b, u_col // ct + c)),
            pl.BlockSpec((t_len, ct), lambda b, c: (row_blk0 + b, g_col // ct + c)),
            pl.BlockSpec((None, SUBLANES, ct), lambda b, c: (b, 0, c)),
            pl.BlockSpec((None, 1, ct), lambda b, c: (b, 0, c)),
            pl.BlockSpec((CONV_W, ct), lambda b, c: (0, c)),
            vec(),
            pl.BlockSpec((nb_blk, LANES, LANES), lambda b, c: (c, 0, 0)),
            vec(),
            pl.BlockSpec((nb_blk, LANES, LANES), lambda b, c: (c, 0, 0)),
            vec(),
            vec(),
        ] + extra_specs,
        out_specs=[
            pl.BlockSpec((t_len, ct), lambda b, c: (row_blk0 + b, c)),
            pl.BlockSpec((None, SUBLANES, ct), lambda b, c: (b, 0, c)),
            pl.BlockSpec((None, 1, ct), lambda b, c: (b, 0, c)),
        ],
        out_shape=[
            jax.ShapeDtypeStruct((y_rows, lru_w), jnp.bfloat16),
            jax.ShapeDtypeStruct((n_b, SUBLANES, lru_w), jnp.float32),
            jax.ShapeDtypeStruct((n_b, 1, lru_w), jnp.float32),
        ],
        scratch_shapes=[
            pltpu.VMEM((t_len + SUBLANES, ct), jnp.float32),
        ],
        compiler_params=_params(("parallel", "parallel")),
        name="rg_lru",
    )(*args)


def _tree(x, op):
    tiles = [x[i:i + SUBLANES] for i in range(0, x.shape[0], SUBLANES)]
    while len(tiles) > 1:
        pairs = [op(tiles[i], tiles[i + 1]) for i in range(0, len(tiles) - 1, 2)]
        tiles = pairs + tiles[len(pairs) * 2:]
    return tiles[0]


def _key_to_float(key):
    return pltpu.bitcast(jnp.where(key < 0, key ^ 0x7FFFFFFF, key), jnp.float32)


def _kth_largest(count_ge, k, lanes):
    thr = jnp.where(count_ge(jnp.zeros((1, lanes), jnp.float32)) >= k, 0, INT_MIN).astype(jnp.int32)

    def bit_body(i, t):
        cand = t + lax.shift_left(jnp.int32(1), 30 - i)
        return jnp.where(count_ge(_key_to_float(cand)) >= k, cand, t)

    thr = lax.fori_loop(0, 31, bit_body, thr)
    return _key_to_float(jnp.maximum(thr, KEY_NEG_INF + 1))


def _n_key_chunks(last_q, s_all):
    n_allowed = jnp.minimum(
        lax.shift_left(lax.shift_right_logical(last_q, CHUNK_SHIFT) + 1, CHUNK_SHIFT), s_all)
    return lax.shift_right_logical(n_allowed + (KEY_BLOCK - 1), KEY_BLOCK_SHIFT)


def _attn_kernel(qi_ref, q_ref, ga_ref, wt_ref, k_ref, v_ref, ki_ref, y_ref,
                 kb_ref, vt_ref, kib_ref, qis_ref, qs_ref, sc_ref, m_ref, s_ref, ot_ref,
                 *, tq, n_sel, n_heads):
    kb = KEY_BLOCK
    group = n_heads // N_KV_HEADS
    s_all = sc_ref.shape[0]
    qb = pl.program_id(1)

    @pl.when(qb == 0)
    def _assemble_keys():
        for blk in range(s_all // LANES):
            rows = slice(blk * LANES, (blk + 1) * LANES)
            ki_f = ki_ref[rows, :]
            ki_hi = ki_f.astype(jnp.bfloat16)
            kib_ref[rows, 0:IDX_DIM] = ki_hi
            kib_ref[rows, IDX_DIM:2 * IDX_DIM] = (ki_f - ki_hi.astype(jnp.float32)).astype(jnp.bfloat16)
            for g in range(N_KV_HEADS):
                cols = slice(g * HEAD_DIM, (g + 1) * HEAD_DIM)
                kb_ref[g, rows, :] = k_ref[rows, cols].astype(jnp.bfloat16)
                lane0 = (blk * LANES) % kb
                vt_ref[g, (blk * LANES) // kb, :, lane0:lane0 + LANES] = (
                    v_ref[rows, cols].T.astype(jnp.bfloat16))

    for h in range(N_IDX_HEADS):
        qi_h = qi_ref[:, h * IDX_DIM:(h + 1) * IDX_DIM].astype(jnp.bfloat16)
        qis_ref[h * tq:(h + 1) * tq, 0:IDX_DIM] = qi_h
        qis_ref[h * tq:(h + 1) * tq, IDX_DIM:2 * IDX_DIM] = qi_h
    q_scale = LOG2_E * HEAD_DIM ** -0.5
    for h in range(n_heads):
        qs_ref[h * tq:(h + 1) * tq, :] = (
            q_ref[:, h * HEAD_DIM:(h + 1) * HEAD_DIM] * q_scale).astype(jnp.bfloat16)

    w = wt_ref[...] * (N_IDX_HEADS ** -0.5)
    qpos = qb * tq + lax.broadcasted_iota(jnp.int32, (1, tq), 1)
    qchunk = lax.shift_right_logical(qpos, CHUNK_SHIFT)
    n_chunks = _n_key_chunks(qb * tq + (tq - 1), s_all)

    def chunk_rows(c):
        return pl.ds(pl.multiple_of(c * kb, kb), kb)

    def score_chunk(c, carry):
        rows = chunk_rows(c)
        kic = kib_ref[rows, :]
        sc = jnp.zeros((kb, tq), jnp.float32)
        for hg in range(N_IDX_HEADS // 4):
            rel = lax.dot_general(kic, qis_ref[hg * 4 * tq:(hg + 1) * 4 * tq, :], _NT,
                                  preferred_element_type=jnp.float32)
            for j in range(4):
                h = hg * 4 + j
                sc = sc + jnp.maximum(rel[:, j * tq:(j + 1) * tq], 0.0) * w[h:h + 1, :]
        sc = sc * (IDX_DIM ** -0.5)
        kpos = c * kb + lax.broadcasted_iota(jnp.int32, (kb, tq), 0)
        allowed = lax.shift_right_logical(kpos, CHUNK_SHIFT) <= qchunk
        sc_ref[rows, :] = jnp.where(allowed, sc, -jnp.inf)
        return carry

    lax.fori_loop(0, n_chunks, score_chunk, 0)

    def count_ge(cand_f):
        def body(c, acc):
            return acc + _tree(jnp.where(sc_ref[chunk_rows(c), :] >= cand_f, 1.0, 0.0), jnp.add)
        acc = lax.fori_loop(0, n_chunks, body, jnp.zeros((SUBLANES, tq), jnp.float32))
        return jnp.sum(acc, axis=0, keepdims=True)

    thr_f = _kth_largest(count_ge, n_sel, tq)

    m_ref[...] = jnp.full(m_ref.shape, -jnp.inf, jnp.float32)
    s_ref[...] = jnp.zeros(s_ref.shape, jnp.float32)
    ot_ref[...] = jnp.zeros(ot_ref.shape, jnp.float32)
    slopes2 = [LOG2_E * 2.0 ** (-8.0 * (h + 1) / n_heads) for h in range(n_heads)]

    def attn_chunk(c, carry):
        rows = chunk_rows(c)
        kpos = c * kb + lax.broadcasted_iota(jnp.int32, (kb, tq), 0)
        dist = jnp.abs(qpos - kpos).astype(jnp.float32)
        dmc = jnp.where(sc_ref[rows, :] >= thr_f, dist, jnp.inf)
        for g in range(N_KV_HEADS):
            qg = qs_ref[g * group * tq:(g + 1) * group * tq, :]
            lg = lax.dot_general(kb_ref[g, rows, :], qg, _NT, preferred_element_type=jnp.float32)
            ps, alphas = [], []
            for j in range(group):
                h = g * group + j
                lj = lg[:, j * tq:(j + 1) * tq] - slopes2[h] * dmc
                m_old = m_ref[h:h + 1, :]
                m_new = jnp.maximum(m_old, jnp.max(_tree(lj, jnp.maximum), axis=0, keepdims=True))
                m_safe = jnp.where(m_new == -jnp.inf, 0.0, m_new)
                alpha = jnp.exp2(m_old - m_safe)
                p = jnp.exp2(lj - m_safe)
                srows = slice(h * SUBLANES, (h + 1) * SUBLANES)
                s_ref[srows, :] = s_ref[srows, :] * alpha + _tree(p, jnp.add)
                m_ref[h:h + 1, :] = m_new
                ps.append(p.astype(jnp.bfloat16))
                alphas.append(alpha)
            pv = jnp.dot(vt_ref[g, c], jnp.concatenate(ps, axis=1), preferred_element_type=jnp.float32)
            ot_ref[g] = ot_ref[g] * jnp.concatenate(alphas, axis=1) + pv
        return carry

    lax.fori_loop(0, n_chunks, attn_chunk, 0)

    for h in range(n_heads):
        g, j = divmod(h, group)
        s = jnp.sum(s_ref[h * SUBLANES:(h + 1) * SUBLANES, :], axis=0, keepdims=True)
        o = (ot_ref[g, :, j * tq:(j + 1) * tq] / s).T
        gate = ga_ref[:, h * HEAD_DIM:(h + 1) * HEAD_DIM]
        y_ref[:, h * HEAD_DIM:(h + 1) * HEAD_DIM] = (o * (gate * _sigmoid(gate))).astype(y_ref.dtype)


def _attn_call(z, ki, wt, y_prev, *, n_b, t_len, tq, cols):
    qi_col, q_col, ga_col, k_col, v_col = cols
    n_heads = (ga_col - q_col) // HEAD_DIM
    att_w = n_heads * HEAD_DIM
    qi_w = N_IDX_HEADS * IDX_DIM
    kv_w = N_KV_HEADS * HEAD_DIM
    n_q = t_len // tq
    n_sel = min(TOPK_MAX, t_len // 4)
    kern = functools.partial(_attn_kernel, tq=tq, n_sel=n_sel, n_heads=n_heads)
    qrow = lambda b, q: b * n_q + q
    n_args = 7
    return pl.pallas_call(
        _drop_ref(kern, n_args),
        grid=(n_b, n_q),
        input_output_aliases={n_args: 0},
        in_specs=[
            pl.BlockSpec((tq, qi_w), lambda b, q: (qrow(b, q), qi_col // qi_w)),
            pl.BlockSpec((tq, att_w), lambda b, q: (qrow(b, q), q_col // att_w)),
            pl.BlockSpec((tq, att_w), lambda b, q: (qrow(b, q), ga_col // att_w)),
            pl.BlockSpec((N_IDX_HEADS, tq), lambda b, q: (0, qrow(b, q))),
            pl.BlockSpec((t_len, kv_w), lambda b, q: (b, k_col // kv_w)),
            pl.BlockSpec((t_len, kv_w), lambda b, q: (b, v_col // kv_w)),
            pl.BlockSpec((t_len, IDX_DIM), lambda b, q: (b, 0)),
            pl.BlockSpec(memory_space=pl.ANY),
        ],
        out_specs=pl.BlockSpec((tq, att_w), lambda b, q: (qrow(b, q), 0)),
        out_shape=jax.ShapeDtypeStruct(y_prev.shape, y_prev.dtype),
        scratch_shapes=[
            pltpu.VMEM((N_KV_HEADS, t_len, HEAD_DIM), jnp.bfloat16),
            pltpu.VMEM((N_KV_HEADS, t_len // KEY_BLOCK, HEAD_DIM, KEY_BLOCK), jnp.bfloat16),
            pltpu.VMEM((t_len, 2 * IDX_DIM), jnp.bfloat16),
            pltpu.VMEM((N_IDX_HEADS * tq, 2 * IDX_DIM), jnp.bfloat16),
            pltpu.VMEM((n_heads * tq, HEAD_DIM), jnp.bfloat16),
            pltpu.VMEM((t_len, tq), jnp.float32),
            pltpu.VMEM((n_heads, tq), jnp.float32),
            pltpu.VMEM((n_heads * SUBLANES, tq), jnp.float32),
            pltpu.VMEM((N_KV_HEADS, HEAD_DIM, (n_heads // N_KV_HEADS) * tq), jnp.float32),
        ],
        compiler_params=_params(("arbitrary", "arbitrary")),
        name="sparse_attn",
    )(z, z, z, wt, z, z, ki, y_prev)


def _decode_attn_kernel(qi_ref, q_ref, ga_ref, wrow_ref, kcur_ref, vcur_ref, kicur_ref,
                        kp_ref, vp_ref, kip_ref, y_ref,
                        kb_ref, vt_ref, kib_ref, qit_ref, bdt_ref, sc_ref, stage_ref,
                        *, t_q, n_past, pos0, n_sel, n_heads):
    kb = KEY_BLOCK
    group = n_heads // N_KV_HEADS
    hpt = LANES // t_q
    n_tiles = n_heads // hpt
    gpt = hpt // group
    s_all = n_past + t_q
    s_pad = sc_ref.shape[0]
    kv_w = N_KV_HEADS * HEAD_DIM
    tile_w = gpt * HEAD_DIM
    slot_shifts = [LANES >> (i + 1) for i in range(hpt.bit_length() - 1)]

    def sum_over_slots(x):
        for sh in slot_shifts:
            x = x + pltpu.roll(x, sh, 1)
        return x

    def load_block(k_src, v_src, ki_src, src_row, dst_row, heads_on_rows):
        rows = slice(src_row, src_row + LANES)
        dst = slice(dst_row, dst_row + LANES)
        ki_f = ki_src[rows, 0:IDX_DIM]
        ki_hi = ki_f.astype(jnp.bfloat16)
        kib_ref[dst, 0:IDX_DIM] = ki_hi
        kib_ref[dst, IDX_DIM:2 * IDX_DIM] = (ki_f - ki_hi.astype(jnp.float32)).astype(jnp.bfloat16)
        for g in range(N_KV_HEADS):
            if heads_on_rows:
                idx = (pl.ds(src_row * N_KV_HEADS + g, LANES, stride=N_KV_HEADS), slice(None))
            else:
                idx = (rows, slice(g * HEAD_DIM, (g + 1) * HEAD_DIM))
            kb_ref[dst, g * HEAD_DIM:(g + 1) * HEAD_DIM] = k_src[idx].astype(jnp.bfloat16)
            lane0 = dst_row % kb
            vt_ref[g // gpt, dst_row // kb, (g % gpt) * HEAD_DIM:(g % gpt + 1) * HEAD_DIM,
                   lane0:lane0 + LANES] = v_src[idx].T.astype(jnp.bfloat16)

    for blk in range(n_past // LANES):
        load_block(kp_ref, vp_ref, kip_ref, blk * LANES, blk * LANES, True)
    stage_ref[...] = jnp.zeros(stage_ref.shape, jnp.float32)
    stage_ref[0, 0:t_q, 0:kv_w] = kcur_ref[...]
    stage_ref[1, 0:t_q, 0:kv_w] = vcur_ref[...]
    stage_ref[2, 0:t_q, 0:IDX_DIM] = kicur_ref[...]
    for blk in range((s_pad - n_past) // LANES):
        load_block(stage_ref.at[0], stage_ref.at[1], stage_ref.at[2], blk * LANES, n_past + blk * LANES,
                   False)

    for h in range(N_IDX_HEADS):
        qi_h = qi_ref[:, h * IDX_DIM:(h + 1) * IDX_DIM].astype(jnp.bfloat16)
        qit_ref[h * t_q:(h + 1) * t_q, 0:IDX_DIM] = qi_h
        qit_ref[h * t_q:(h + 1) * t_q, IDX_DIM:2 * IDX_DIM] = qi_h
    bdt_ref[...] = jnp.zeros(bdt_ref.shape, bdt_ref.dtype)
    q_scale = LOG2_E * HEAD_DIM ** -0.5
    for h in range(n_heads):
        tile, slot = divmod(h, hpt)
        gl = slot // group
        bdt_ref[tile, slot * t_q:(slot + 1) * t_q, gl * HEAD_DIM:(gl + 1) * HEAD_DIM] = (
            q_ref[:, h * HEAD_DIM:(h + 1) * HEAD_DIM] * q_scale).astype(jnp.bfloat16)

    lane = lax.broadcasted_iota(jnp.int32, (1, LANES), 1)
    slot_l = lax.shift_right_logical(lane, t_q.bit_length() - 1)
    qpos = pos0 + (lane & (t_q - 1))
    qchunk = lax.shift_right_logical(qpos, CHUNK_SHIFT)
    w = wrow_ref[...] * (N_IDX_HEADS ** -0.5)
    n_chunks = s_pad // kb

    for c in range(n_chunks):
        rows = slice(c * kb, (c + 1) * kb)
        rel = lax.dot_general(kib_ref[rows, :], qit_ref[...], _NT, preferred_element_type=jnp.float32)
        x = jnp.maximum(rel, 0.0) * w
        acc = x[:, 0:LANES]
        for i in range(1, x.shape[1] // LANES):
            acc = acc + x[:, i * LANES:(i + 1) * LANES]
        sc = sum_over_slots(acc) * (IDX_DIM ** -0.5)
        kpos = c * kb + lax.broadcasted_iota(jnp.int32, (kb, LANES), 0)
        allowed = (lax.shift_right_logical(kpos, CHUNK_SHIFT) <= qchunk) & (kpos < s_all)
        sc_ref[rows, :] = jnp.where(allowed, sc, -jnp.inf)

    def count_ge(cand_f):
        return jnp.sum(_tree(jnp.where(sc_ref[...] >= cand_f, 1.0, 0.0), jnp.add), axis=0, keepdims=True)

    thr_f = _kth_largest(count_ge, n_sel, LANES)

    slopes2 = []
    for p in range(n_tiles):
        head = (p * hpt + slot_l + 1).astype(jnp.float32)
        slopes2.append(LOG2_E * jnp.exp2(head * (-8.0 / n_heads)))
    m = [jnp.full((1, LANES), -jnp.inf, jnp.float32) for _ in range(n_tiles)]
    s8 = [jnp.zeros((SUBLANES, LANES), jnp.float32) for _ in range(n_tiles)]
    ot = [jnp.zeros((tile_w, LANES), jnp.float32) for _ in range(n_tiles)]
    for c in range(n_chunks):
        rows = slice(c * kb, (c + 1) * kb)
        kpos = c * kb + lax.broadcasted_iota(jnp.int32, (kb, LANES), 0)
        dist = jnp.abs(qpos - kpos).astype(jnp.float32)
        dmc = jnp.where(sc_ref[rows, :] >= thr_f, dist, jnp.inf)
        for p in range(n_tiles):
            lg = lax.dot_general(kb_ref[rows, p * tile_w:(p + 1) * tile_w], bdt_ref[p], _NT,
                                 preferred_element_type=jnp.float32)
            lj = lg - slopes2[p] * dmc
            m_new = jnp.maximum(m[p], jnp.max(_tree(lj, jnp.maximum), axis=0, keepdims=True))
            m_safe = jnp.where(m_new == -jnp.inf, 0.0, m_new)
            alpha = jnp.exp2(m[p] - m_safe)
            pr = jnp.exp2(lj - m_safe)
            s8[p] = s8[p] * alpha + _tree(pr, jnp.add)
            m[p] = m_new
            pv = jnp.dot(vt_ref[p, c], pr.astype(jnp.bfloat16), preferred_element_type=jnp.float32)
            ot[p] = ot[p] * alpha + pv

    for p in range(n_tiles):
        o = ot[p] / jnp.sum(s8[p], axis=0, keepdims=True)
        for gl in range(gpt):
            o_t = o[gl * HEAD_DIM:(gl + 1) * HEAD_DIM, :].T
            for j in range(group):
                slot = gl * group + j
                h = p * hpt + slot
                gate = ga_ref[:, h * HEAD_DIM:(h + 1) * HEAD_DIM]
                y_ref[:, h * HEAD_DIM:(h + 1) * HEAD_DIM] = (
                    o_t[slot * t_q:(slot + 1) * t_q, :] * (gate * _sigmoid(gate))).astype(y_ref.dtype)


def _decode_attn_call(z, ki, wrow, past, y_prev, *, layer, row_blk0, n_b, t_q, pos0, cols):
    qi_col, q_col, ga_col, k_col, v_col = cols
    n_heads = (ga_col - q_col) // HEAD_DIM
    att_w = n_heads * HEAD_DIM
    qi_w = N_IDX_HEADS * IDX_DIM
    kv_w = N_KV_HEADS * HEAD_DIM
    group = n_heads // N_KV_HEADS
    hpt = LANES // t_q
    assert t_q & (t_q - 1) == 0 and LANES % t_q == 0 and n_heads % hpt == 0 and hpt % group == 0
    assert (N_IDX_HEADS * t_q) % LANES == 0
    n_past = past[2].shape[2]
    s_all = n_past + t_q
    cur_pad = ((t_q + KEY_BLOCK - 1) // KEY_BLOCK) * KEY_BLOCK
    s_pad = n_past + cur_pad
    n_sel = min(TOPK_MAX, s_all // 4)
    n_tiles = n_heads // hpt
    gpt = hpt // group
    kern = functools.partial(_decode_attn_kernel, t_q=t_q, n_past=n_past, pos0=pos0, n_sel=n_sel,
                             n_heads=n_heads)
    row = lambda b: row_blk0 + b
    n_args = 10
    return pl.pallas_call(
        _drop_ref(kern, n_args),
        grid=(n_b,),
        input_output_aliases={n_args: 0},
        in_specs=[
            pl.BlockSpec((t_q, qi_w), lambda b: (row(b), qi_col // qi_w)),
            pl.BlockSpec((t_q, att_w), lambda b: (row(b), q_col // att_w)),
            pl.BlockSpec((t_q, att_w), lambda b: (row(b), ga_col // att_w)),
            pl.BlockSpec((None, 1, N_IDX_HEADS * t_q), lambda b: (b, 0, 0)),
            pl.BlockSpec((t_q, kv_w), lambda b: (row(b), k_col // kv_w)),
            pl.BlockSpec((t_q, kv_w), lambda b: (row(b), v_col // kv_w)),
            pl.BlockSpec((t_q, IDX_DIM), lambda b: (row(b), 0)),
            pl.BlockSpec((None, None, n_past * N_KV_HEADS, HEAD_DIM), lambda b: (layer, b, 0, 0)),
            pl.BlockSpec((None, None, n_past * N_KV_HEADS, HEAD_DIM), lambda b: (layer, b, 0, 0)),
            pl.BlockSpec((None, None, n_past, IDX_DIM), lambda b: (layer, b, 0, 0)),
            pl.BlockSpec(memory_space=pl.ANY),
        ],
        out_specs=pl.BlockSpec((t_q, att_w), lambda b: (row(b), 0)),
        out_shape=jax.ShapeDtypeStruct(y_prev.shape, y_prev.dtype),
        scratch_shapes=[
            pltpu.VMEM((s_pad, kv_w), jnp.bfloat16),
            pltpu.VMEM((N_KV_HEADS // gpt, s_pad // KEY_BLOCK, gpt * HEAD_DIM, KEY_BLOCK), jnp.bfloat16),
            pltpu.VMEM((s_pad, 2 * IDX_DIM), jnp.bfloat16),
            pltpu.VMEM((N_IDX_HEADS * t_q, 2 * IDX_DIM), jnp.bfloat16),
            pltpu.VMEM((n_tiles, LANES, gpt * HEAD_DIM), jnp.bfloat16),
            pltpu.VMEM((s_pad, LANES), jnp.float32),
            pltpu.VMEM((3, cur_pad, kv_w), jnp.float32),
        ],
        compiler_params=_params(("parallel",)),
        name="decode_attn",
    )(z, z, z, wrow, z, z, ki, *past, y_prev)


def _out_kernel(yl_ref, ya_ref, xb_ref, p_ref, xf_ref, wo_ref, wg_ref, wp_ref, o_ref, *, alpha):
    half = yl_ref.shape[1]
    mix = (jnp.dot(yl_ref[...], wo_ref[0:half, :], preferred_element_type=jnp.float32)
           + jnp.dot(ya_ref[...], wo_ref[half:, :], preferred_element_type=jnp.float32))
    gate = jax.nn.sigmoid(jnp.dot(xb_ref[...], wg_ref[...], preferred_element_type=jnp.float32))
    emb = jnp.dot(p_ref[...], wp_ref[...], preferred_element_type=jnp.float32)
    o_ref[...] = alpha * xf_ref[...] + mix + gate * emb


def _out_call(yl, ya, xb, p, xf, wo, wg, wp, layer, alpha, tm, tn):
    m, d = xf.shape
    half = yl.shape[1]
    ple = p.shape[1]
    row = lambda w: pl.BlockSpec((tm, w), lambda i, j: (i, 0))
    col = lambda k: pl.BlockSpec((k, tn), lambda i, j: (0, j))
    lcol = lambda k: pl.BlockSpec((None, k, tn), lambda i, j: (layer, 0, j))
    return pl.pallas_call(
        functools.partial(_out_kernel, alpha=alpha),
        grid=(m // tm, d // tn),
        in_specs=[row(half), row(ya.shape[1]), row(d), row(ple),
                  pl.BlockSpec((tm, tn), lambda i, j: (i, j)),
                  lcol(wo.shape[1]), lcol(d), col(ple)],
        out_specs=pl.BlockSpec((tm, tn), lambda i, j: (i, j)),
        out_shape=jax.ShapeDtypeStruct((m, d), jnp.float32),
        compiler_params=_params(("parallel", "parallel")),
        name="out_proj",
    )(yl, ya, xb, p, xf, wo, wg, wp)


def _ln_rows(x_ref, g_ref, b_ref, out_refs, rs):
    def sub_block(c, carry):
        rows = pl.ds(pl.multiple_of(c * rs, rs), rs)
        x = x_ref[rows, :]
        mu = jnp.mean(x, axis=-1, keepdims=True)
        xc = x - mu
        var = jnp.mean(xc * xc, axis=-1, keepdims=True)
        y = xc * lax.rsqrt(var + LN_EPS) * g_ref[...] + b_ref[...]
        for o_ref in out_refs:
            o_ref[rows, :] = y.astype(o_ref.dtype)
        return carry

    lax.fori_loop(0, x_ref.shape[0] // rs, sub_block, 0)


def _ln_kernel(x_ref, g_ref, b_ref, of_ref, ob_ref, *, rs):
    _ln_rows(x_ref, g_ref, b_ref, (of_ref, ob_ref), rs)


def _ln_split_kernel(x_ref, g_ref, b_ref, head_ref, tail_ref, *, rs, n_head):
    i = pl.program_id(0)

    @pl.when(i < n_head)
    def _head():
        _ln_rows(x_ref, g_ref, b_ref, (head_ref,), rs)

    @pl.when(i >= n_head)
    def _tail():
        _ln_rows(x_ref, g_ref, b_ref, (tail_ref,), rs)


def _ln_split_call(x, g, b, tr, m_head):
    m, d = x.shape
    n_head = m_head // tr
    return pl.pallas_call(
        functools.partial(_ln_split_kernel, rs=_pick_tile(tr, (32, 16)), n_head=n_head),
        grid=(m // tr,),
        in_specs=[pl.BlockSpec((tr, d), lambda i: (i, 0)),
                  pl.BlockSpec((1, d), lambda i: (0, 0)),
                  pl.BlockSpec((1, d), lambda i: (0, 0))],
        out_specs=[pl.BlockSpec((tr, d), lambda i: (jnp.minimum(i, n_head - 1), 0)),
                   pl.BlockSpec((tr, d), lambda i: (jnp.maximum(i - n_head, 0), 0))],
        out_shape=[jax.ShapeDtypeStruct((m_head, d), jnp.float32),
                   jax.ShapeDtypeStruct((m - m_head, d), jnp.float32)],
        compiler_params=_params(("arbitrary",)),
        name="layer_norm_out",
    )(x, g, b)


def _ln_call(x, g, b, tr):
    m, d = x.shape
    return pl.pallas_call(
        functools.partial(_ln_kernel, rs=_pick_tile(tr, (32, 16))),
        grid=(m // tr,),
        in_specs=[pl.BlockSpec((tr, d), lambda i: (i, 0)),
                  pl.BlockSpec((1, d), lambda i: (0, 0)),
                  pl.BlockSpec((1, d), lambda i: (0, 0))],
        out_specs=[pl.BlockSpec((tr, d), lambda i: (i, 0)),
                   pl.BlockSpec((tr, d), lambda i: (i, 0))],
        out_shape=[jax.ShapeDtypeStruct((m, d), jnp.float32),
                   jax.ShapeDtypeStruct((m, d), jnp.bfloat16)],
        compiler_params=_params(("parallel",)),
        name="layer_norm",
    )(x, g, b)


def kernel(x_prompt, x_sample, cache_k, cache_v, cache_kidx, state_conv, state_h, p_prompt, p_sample, w_in, conv_w, conv_b, gate_a_w, gate_a_b, gate_x_w, gate_x_b, lru_lambda, w_out, ln_g, ln_b, ple_proj, ple_gate):
    bf16 = jnp.bfloat16
    n_bp, t_p, d = x_prompt.shape
    n_bs, t_s, _ = x_sample.shape
    depth = w_in.shape[0]
    n_past = cache_k.shape[2]
    lru_w = conv_w.shape[2]
    att_w = d - lru_w
    n_heads = att_w // HEAD_DIM
    kv_w = N_KV_HEADS * HEAD_DIM
    qi_w = N_IDX_HEADS * IDX_DIM
    alpha = (2 * depth) ** 0.25
    m_p, m_s = n_bp * t_p, n_bs * t_s
    m = m_p + m_s
    assert t_p % KEY_BLOCK == 0 and m_p % t_s == 0 and t_s % SUBLANES == 0 and n_past % KEY_BLOCK == 0
    assert lru_w // N_LRU_BLOCKS == LANES and CHUNK == 1 << CHUNK_SHIFT

    o_u, o_gl = 0, lru_w
    o_q = 2 * lru_w
    o_k = o_q + att_w
    o_v = o_k + kv_w
    o_ga = o_v + kv_w
    o_qi = o_ga + att_w
    o_ki = o_qi + qi_w
    c_qi, c_q = 0, qi_w
    c_ga = c_q + att_w
    c_u = c_ga + att_w
    c_gl = c_u + lru_w
    c_k = c_gl + lru_w
    c_v = c_k + kv_w
    n_main = c_v + kv_w

    tm = _pick_tile(m, (768, 512, 384, 256, 128))
    tm_in = _pick_tile(m, (1056, 768, 512, 384, 256, 128))
    tn_in = _pick_tile(n_main, (1024, 512))
    tn_out = _pick_tile(d, (512,))
    tr_ln = _pick_tile(m, (256, 128))
    tr_out = [c for c in (256, 128, 64, 32, 16, 8) if m_p % c == 0 and m_s % c == 0][0]
    ct_p = _pick_tile(lru_w, (512, 256))
    tq = _pick_tile(t_p, (256,))

    xf, xb = _pack_rows_call(x_prompt.reshape(m_p, d), x_sample.reshape(m_s, d), tr_out)
    hist_pad = ((0, 0), (SUBLANES - (CONV_W - 1), 0), (0, 0))
    zero_conv = jnp.zeros((n_bp, SUBLANES, lru_w), jnp.float32)
    zero_h = jnp.zeros((n_bp, 1, lru_w), jnp.float32)
    past = (cache_k.reshape(depth, n_bs, n_past * N_KV_HEADS, HEAD_DIM),
            cache_v.reshape(depth, n_bs, n_past * N_KV_HEADS, HEAD_DIM), cache_kidx)

    yl = jnp.zeros((m, lru_w), bf16)
    ya = jnp.zeros((m, att_w), bf16)
    cast_rows = _pick_tile(d, (256, 128))
    w_out16 = _cast_call(w_out, cast_rows)
    ple_gate16 = _cast_call(ple_gate, cast_rows)
    outs_p, outs_s = [], []
    for i in range(depth):
        wl = w_in[i]
        seg = lambda o, n: wl[:, o:o + n]
        w_main = jnp.concatenate(
            [seg(o_qi, qi_w), seg(o_q, att_w), seg(o_ga, att_w), seg(o_u, lru_w), seg(o_gl, lru_w),
             seg(o_k, kv_w), seg(o_v, kv_w)], axis=1).astype(bf16)
        w_side = wl[:, o_ki:].astype(bf16)

        z, k_rows, v_rows = _in_proj_call(xb, w_main, tm_in, tn_in, c_k)
        ki, wi = _idx_proj_call(xb, w_side, tm)
        wt_p = wi[:m_p].T
        wrow_s = wi[m_p:].reshape(n_bs, t_s, N_IDX_HEADS).transpose(0, 2, 1).reshape(n_bs, 1, -1)

        lru_args = (conv_w[i], conv_b[i][None], gate_a_w[i].astype(bf16), gate_a_b[i][None],
                    gate_x_w[i].astype(bf16), gate_x_b[i][None], lru_lambda[i][None])
        yl, conv_p, h_p = _lru_call(z, 0, n_bp, t_p, ct_p, c_u, c_gl, zero_conv, zero_h, *lru_args,
                                    pos0=0, y_rows=m, y_prev=yl)
        yl, conv_s, h_s = _lru_call(z, m_p // t_s, n_bs, t_s, lru_w, c_u, c_gl,
                                    jnp.pad(state_conv[i], hist_pad), state_h[i][:, None, :], *lru_args,
                                    pos0=n_past, y_rows=m, y_prev=yl)
        conv_p, conv_s = conv_p[:, :CONV_W - 1], conv_s[:, :CONV_W - 1]

        cols = (c_qi, c_q, c_ga, c_k, c_v)
        ya = _attn_call(z, ki, wt_p, ya, n_b=n_bp, t_len=t_p, tq=tq, cols=cols)
        ya = _decode_attn_call(z, ki, wrow_s, past, ya, layer=i, row_blk0=m_p // t_s, n_b=n_bs, t_q=t_s,
                               pos0=n_past, cols=cols)

        p = jnp.concatenate([p_prompt[i].reshape(m_p, -1), p_sample[i].reshape(m_s, -1)], axis=0).astype(bf16)
        pre = _out_call(yl, ya, xb, p, xf, w_out16, ple_gate16, ple_proj[i].astype(bf16), i, alpha, tm,
                        tn_out)
        if i + 1 < depth:
            xf, xb = _ln_call(pre, ln_g[i][None], ln_b[i][None], tr_ln)
        else:
            y_p, y_s = _ln_split_call(pre, ln_g[i][None], ln_b[i][None], tr_out, m_p)

        hp = m_p * N_KV_HEADS
        outs_p.append((k_rows[:hp].reshape(n_bp, t_p, N_KV_HEADS, HEAD_DIM),
                       v_rows[:hp].reshape(n_bp, t_p, N_KV_HEADS, HEAD_DIM),
                       ki[:m_p].reshape(n_bp, t_p, IDX_DIM), conv_p, h_p[:, 0, :]))
        outs_s.append((k_rows[hp:].reshape(n_bs, t_s, N_KV_HEADS, HEAD_DIM),
                       v_rows[hp:].reshape(n_bs, t_s, N_KV_HEADS, HEAD_DIM),
                       ki[m_p:].reshape(n_bs, t_s, IDX_DIM), conv_s, h_s[:, 0, :]))

    stack = lambda outs, j: jnp.stack([o[j] for o in outs])
    return (y_p.reshape(n_bp, t_p, d), y_s.reshape(n_bs, t_s, d),
            stack(outs_p, 0), stack(outs_p, 1), stack(outs_p, 2), stack(outs_p, 3), stack(outs_p, 4),
            stack(outs_s, 0), stack(outs_s, 1), stack(outs_s, 2), stack(outs_s, 3), stack(outs_s, 4))
```
